```python
import math, functools
import jax, jax.numpy as jnp
from jax import lax
import numpy as np

D_MODEL = 1024
BATCH = 8
SEQ = 2048
DEPTH = 4
DEC_BATCH = 32
DEC_SEQ = 8
PAST_LEN = 8192
PAGE_SIZE = 128

HEAD_DIM = 64
N_HEADS_TOTAL = D_MODEL // HEAD_DIM
H_B = N_HEADS_TOTAL // 4
H_A = (N_HEADS_TOTAL - H_B) // 2
H_C = N_HEADS_TOTAL - H_A - H_B
HD_B = HEAD_DIM // 2
N_IDX_HEADS = 4
IDX_DIM = 64
TOPK_MAX = 256
NUM_BUCKETS = 32
MAX_DISTANCE = 128
Q_BLOCK = 128
D_MIX = (H_A + H_C) * HEAD_DIM + H_B * 2 * HD_B
D_FF = ((8 * D_MODEL + 3 * 256 - 1) // (3 * 256)) * 256
EPS = 1e-6
NEG = -1e30
IN_SPLITS = (H_A * HEAD_DIM, H_A * HEAD_DIM, H_A * HEAD_DIM,
             N_IDX_HEADS * IDX_DIM, N_IDX_HEADS, IDX_DIM,
             H_B * 2 * HD_B, H_B * 2 * HD_B, H_B * 2 * HD_B,
             H_C * HEAD_DIM, H_C * HEAD_DIM, H_C * HEAD_DIM)
D_IN = sum(IN_SPLITS)

kernel_name = 'hybrid_dsa_diff_stickbreak_decoder_step'


def rmsnorm(x, g):
    xf = x.astype(jnp.float32)
    return xf * lax.rsqrt(jnp.mean(xf * xf, axis=-1, keepdims=True) + EPS) * g


def t5_bucket(rel):
    n = jnp.maximum(rel, 0)
    max_exact = NUM_BUCKETS // 2
    nf = jnp.maximum(n, 1).astype(jnp.float32)
    large = max_exact + (jnp.log(nf / max_exact) / math.log(MAX_DISTANCE / max_exact)
                         * (NUM_BUCKETS - max_exact)).astype(jnp.int32)
    large = jnp.minimum(large, NUM_BUCKETS - 1)
    return jnp.where(n < max_exact, n, large)


def to_blocks(x, qb):
    return jnp.moveaxis(x.reshape((x.shape[0], x.shape[1] // qb, qb) + x.shape[2:]), 1, 0)


def from_blocks(y):
    y = jnp.moveaxis(y, 0, 1)
    return y.reshape((y.shape[0], y.shape[1] * y.shape[2]) + y.shape[3:])


def gather_rows(src, idx):
    return jax.vmap(lambda s, i: s[i])(src, idx)


def gather_pages(pool, page_table):
    g = pool[page_table]
    return g.reshape((g.shape[0], g.shape[1] * g.shape[2]) + g.shape[3:])


def with_past(pool, page_table, new):
    return jnp.concatenate([gather_pages(pool, page_table), new], axis=1)


def gather_new_rows(k_new, v_new, idx):
    return gather_rows(k_new, idx), gather_rows(v_new, idx)


def gather_paged_or_new(pool_k, pool_v, k_new, v_new, page_table, idx):
    page_size = pool_k.shape[1]
    past_len = page_table.shape[1] * page_size
    is_past = (idx < past_len)[..., None, None]
    ip = jnp.minimum(idx, past_len - 1)
    phys = gather_rows(page_table, ip // page_size)
    off = ip % page_size
    inew = jnp.clip(idx - past_len, 0, k_new.shape[1] - 1)
    k_sel = jnp.where(is_past, pool_k[phys, off], gather_rows(k_new, inew))
    v_sel = jnp.where(is_past, pool_v[phys, off], gather_rows(v_new, inew))
    return k_sel, v_sel


def dsa_attention(q, qi, wi, ki_all, q_pos, gather_kv, bias_a):
    T = q.shape[1]
    L = ki_all.shape[1]
    topk = min(TOPK_MAX, L // 4)
    qb = math.gcd(T, Q_BLOCK)
    k_pos = jnp.arange(L, dtype=jnp.int32)

    def block(args):
        q_b, qi_b, wi_b, p_b = args
        rel = jax.nn.relu(jnp.einsum('bqnd,bsd->bqns', qi_b, ki_all).astype(jnp.float32) * IDX_DIM ** -0.5)
        score = jnp.einsum('bqns,bqn->bqs', rel, wi_b.astype(jnp.float32)) * N_IDX_HEADS ** -0.5
        score = jnp.where(k_pos[None, None, :] <= p_b[None, :, None], score, NEG)
        _, idx = lax.top_k(score, topk)
        valid = idx <= p_b[None, :, None]
        k_sel, v_sel = gather_kv(idx)
        logits = jnp.einsum('bqhd,bqkhd->bqhk', q_b, k_sel).astype(jnp.float32) * HEAD_DIM ** -0.5
        bias = bias_a[t5_bucket(p_b[None, :, None] - idx)]
        logits = jnp.where(valid[:, :, None, :], logits + jnp.swapaxes(bias, -1, -2), NEG)
        p = jax.nn.softmax(logits, axis=-1)
        return jnp.einsum('bqhk,bqkhd->bqhd', p, v_sel)

    out = lax.map(block, (to_blocks(q, qb), to_blocks(qi, qb), to_blocks(wi, qb), q_pos.reshape(-1, qb)))
    return from_blocks(out)


def diff_attention(q, k_all, v_all, q_pos, lam, lam_init, sub_g, bias_b):
    T = q.shape[1]
    L = k_all.shape[1]
    qb = math.gcd(T, Q_BLOCK)
    k_pos = jnp.arange(L, dtype=jnp.int32)

    def block(args):
        q_b, p_b = args
        logits = jnp.einsum('bqhnd,bshnd->bnhqs', q_b, k_all).astype(jnp.float32) * HD_B ** -0.5
        rel = p_b[:, None] - k_pos[None, :]
        bias = jnp.transpose(bias_b[t5_bucket(rel)], (2, 0, 1))
        logits = jnp.where((rel >= 0)[None, None, None], logits + bias[None, None], NEG)
        p = jax.nn.softmax(logits, axis=-1)
        attn = p[:, 0] - lam * p[:, 1]
        o = jnp.einsum('bhqs,bshe->bqhe', attn, v_all)
        return rmsnorm(o, sub_g) * (1.0 - lam_init)

    out = lax.map(block, (to_blocks(q, qb), q_pos.reshape(-1, qb)))
    return from_blocks(out)


def stick_breaking_attention(q, k_all, v_all, q_pos):
    T = q.shape[1]
    L = k_all.shape[1]
    qb = math.gcd(T, Q_BLOCK)
    k_pos = jnp.arange(L, dtype=jnp.int32)

    def block(args):
        q_b, p_b = args
        z = jnp.einsum('bqhd,bshd->bhqs', q_b, k_all).astype(jnp.float32) * HEAD_DIM ** -0.5
        strict = (k_pos[None, :] < p_b[:, None])[None, None]
        log_beta = jax.nn.log_sigmoid(z)
        log_keep = jnp.where(strict, jax.nn.log_sigmoid(-z), 0.0)
        later = lax.cumsum(log_keep, axis=3, reverse=True) - log_keep
        a = jnp.where(strict, jnp.exp(log_beta + later), 0.0)
        return jnp.einsum('bhqs,bshd->bqhd', a, v_all)

    out = lax.map(block, (to_blocks(q, qb), q_pos.reshape(-1, qb)))
    return from_blocks(out)


def run_trunk(x, c, pos0, past, page_table, rel_bias, w_mod, b_mod, g_attn, g_ffn, w_in,
              diff_lambda, subln_g, w_out, w_ffn_in, w_ffn_out, g_final):
    B, T, _ = x.shape
    q_pos = pos0 + jnp.arange(T, dtype=jnp.int32)
    offsets = np.cumsum(IN_SPLITS)[:-1].tolist()
    bias_a = rel_bias[:, :H_A]
    bias_b = rel_bias[:, H_A:]
    c_act = jax.nn.silu(c.astype(jnp.float32))
    rows = [[] for _ in range(7)]
    for l in range(DEPTH):
        lam_init = 0.8 - 0.6 * math.exp(-0.3 * l)
        mod = c_act @ w_mod[l] + b_mod[l]
        sh1, sc1, gt1, sh2, sc2, gt2 = jnp.split(mod[:, None, :], 6, axis=-1)
        h = rmsnorm(x, g_attn[l]) * (1.0 + sc1) + sh1
        qa, ka, va, qi, wi, ki, qd, kd, vd, qc, kc, vc = jnp.split(h @ w_in[l], offsets, axis=-1)
        qa = qa.reshape(B, T, H_A, HEAD_DIM)
        ka = ka.reshape(B, T, H_A, HEAD_DIM)
        va = va.reshape(B, T, H_A, HEAD_DIM)
        qi = qi.reshape(B, T, N_IDX_HEADS, IDX_DIM)
        qd = qd.reshape(B, T, H_B, 2, HD_B)
        kd = kd.reshape(B, T, H_B, 2, HD_B)
        vd = vd.reshape(B, T, H_B, 2 * HD_B)
        qc = qc.reshape(B, T, H_C, HEAD_DIM)
        kc = kc.reshape(B, T, H_C, HEAD_DIM)
        vc = vc.reshape(B, T, H_C, HEAD_DIM)
        for r, n in zip(rows, (ka, va, ki, kd, vd, kc, vc)):
            r.append(n)
        if past is None:
            ki_all, kd_all, vd_all, kc_all, vc_all = ki, kd, vd, kc, vc
            gather_kv = functools.partial(gather_new_rows, ka, va)
        else:
            pa_k, pa_v, pa_ki, pd_k, pd_v, pc_k, pc_v = [p[l] for p in past]
            ki_all = with_past(pa_ki, page_table, ki)
            kd_all = with_past(pd_k, page_table, kd)
            vd_all = with_past(pd_v, page_table, vd)
            kc_all = with_past(pc_k, page_table, kc)
            vc_all = with_past(pc_v, page_table, vc)
            gather_kv = functools.partial(gather_paged_or_new, pa_k, pa_v, ka, va, page_table)
        dl = diff_lambda[l].astype(jnp.float32)
        lam = jnp.exp(jnp.sum(dl[0] * dl[1])) - jnp.exp(jnp.sum(dl[2] * dl[3])) + lam_init
        oa = dsa_attention(qa, qi, wi, ki_all, q_pos, gather_kv, bias_a).reshape(B, T, -1)
        ob = diff_attention(qd, kd_all, vd_all, q_pos, lam, lam_init, subln_g[l], bias_b).reshape(B, T, -1)
        oc = stick_breaking_attention(qc, kc_all, vc_all, q_pos).reshape(B, T, -1)
        x = x + gt1 * (jnp.concatenate([oa, ob, oc], axis=-1) @ w_out[l])
        h2 = rmsnorm(x, g_ffn[l]) * (1.0 + sc2) + sh2
        gate, up = jnp.split(h2 @ w_ffn_in[l], 2, axis=-1)
        x = x + gt2 * ((jax.nn.silu(gate) * up) @ w_ffn_out[l])
    y = rmsnorm(x, g_final)
    return y, [jnp.stack(r) for r in rows]


def setup_inputs(seed: int = 0) -> dict:
    key = jax.random.key(seed)
    ks = jax.random.split(key, 24)
    n_pages = PAST_LEN // PAGE_SIZE
    n_pool = (DEC_BATCH * n_pages * 5 + 3) // 4

    def nrm(k, shape, scale=1.0):
        return jax.random.normal(k, shape, jnp.float32) * scale

    page_table = jax.random.permutation(ks[11], n_pool)[: DEC_BATCH * n_pages]
    page_table = page_table.reshape(DEC_BATCH, n_pages).astype(jnp.int32)
    return {
        'x_prompt': nrm(ks[0], (BATCH, SEQ, D_MODEL)),
        'x_sample': nrm(ks[1], (DEC_BATCH, DEC_SEQ, D_MODEL)),
        'c_prompt': nrm(ks[2], (BATCH, D_MODEL)),
        'c_sample': nrm(ks[3], (DEC_BATCH, D_MODEL)),
        'cache_a_k': nrm(ks[4], (DEPTH, n_pool, PAGE_SIZE, H_A, HEAD_DIM)),
        'cache_a_v': nrm(ks[5], (DEPTH, n_pool, PAGE_SIZE, H_A, HEAD_DIM)),
        'cache_a_kidx': nrm(ks[6], (DEPTH, n_pool, PAGE_SIZE, IDX_DIM)),
        'cache_b_k': nrm(ks[7], (DEPTH, n_pool, PAGE_SIZE, H_B, 2, HD_B)),
        'cache_b_v': nrm(ks[8], (DEPTH, n_pool, PAGE_SIZE, H_B, 2 * HD_B)),
        'cache_c_k': nrm(ks[9], (DEPTH, n_pool, PAGE_SIZE, H_C, HEAD_DIM)),
        'cache_c_v': nrm(ks[10], (DEPTH, n_pool, PAGE_SIZE, H_C, HEAD_DIM)),
        'page_table': page_table,
        'rel_bias': nrm(ks[12], (NUM_BUCKETS, H_A + H_B), 0.5),
        'w_mod': nrm(ks[13], (DEPTH, D_MODEL, 6 * D_MODEL), 0.3 * D_MODEL ** -0.5),
        'b_mod': nrm(ks[14], (DEPTH, 6 * D_MODEL), 0.02),
        'g_attn': 1.0 + nrm(ks[15], (DEPTH, D_MODEL), 0.02),
        'g_ffn': 1.0 + nrm(ks[16], (DEPTH, D_MODEL), 0.02),
        'w_in': nrm(ks[17], (DEPTH, D_MODEL, D_IN), D_MODEL ** -0.5),
        'diff_lambda': nrm(ks[18], (DEPTH, 4, HD_B), 0.1),
        'subln_g': 1.0 + nrm(ks[19], (DEPTH, 2 * HD_B), 0.02),
        'w_out': nrm(ks[20], (DEPTH, D_MIX, D_MODEL), D_MIX ** -0.5),
        'w_ffn_in': nrm(ks[21], (DEPTH, D_MODEL, 2 * D_FF), D_MODEL ** -0.5),
        'w_ffn_out': nrm(ks[22], (DEPTH, D_FF, D_MODEL), D_FF ** -0.5),
        'g_final': 1.0 + nrm(ks[23], (D_MODEL,), 0.02),
    }


def reference(x_prompt, x_sample, c_prompt, c_sample, cache_a_k, cache_a_v, cache_a_kidx,
              cache_b_k, cache_b_v, cache_c_k, cache_c_v, page_table, rel_bias, w_mod, b_mod,
              g_attn, g_ffn, w_in, diff_lambda, subln_g, w_out, w_ffn_in, w_ffn_out, g_final):
    y_prompt, rows_p = run_trunk(x_prompt, c_prompt, 0, None, None, rel_bias, w_mod, b_mod, g_attn,
                                 g_ffn, w_in, diff_lambda, subln_g, w_out, w_ffn_in, w_ffn_out, g_final)
    past = (cache_a_k, cache_a_v, cache_a_kidx, cache_b_k, cache_b_v, cache_c_k, cache_c_v)
    pos0 = page_table.shape[1] * cache_a_k.shape[2]
    y_sample, rows_s = run_trunk(x_sample, c_sample, pos0, past, page_table, rel_bias, w_mod, b_mod,
                                 g_attn, g_ffn, w_in, diff_lambda, subln_g, w_out, w_ffn_in,
                                 w_ffn_out, g_final)
    ak_p, av_p, aki_p, bk_p, bv_p, ck_p, cv_p = rows_p
    ak_s, av_s, aki_s, bk_s, bv_s, ck_s, cv_s = rows_s
    return (y_prompt, y_sample, ak_p, av_p, aki_p, bk_p, bv_p, ck_p, cv_p,
            ak_s, av_s, aki_s, bk_s, bv_s, ck_s, cv_s)
```

```python
import functools
import math

import numpy as np
import jax
import jax.numpy as jnp
from jax import lax
from jax.experimental import pallas as pl
from jax.experimental.pallas import tpu as pltpu

F32, BF16, I32 = jnp.float32, jnp.bfloat16, jnp.int32

HEAD_DIM = 64
H_A, H_B, H_C = 6, 4, 6
HD_B = HEAD_DIM // 2
N_IDX_HEADS = 4
IDX_DIM = 64
TOPK_MAX = 256
NUM_BUCKETS = 32
MAX_DISTANCE = 128
EPS = 1e-6
NEG = -1e30
M_INIT = -3e38
INT_MIN = -2 ** 31
LANES = 128
BLK = 128
SQ = 16
VMEM_LIMIT = 48 * 1024 * 1024

_PROJ = {}
_off = 0
for _n, _w in (("qa", 384), ("ka", 384), ("va", 384), ("qi", 256), ("wi", 128), ("ki2", 128),
               ("qd", 256), ("kd", 256), ("vd", 256), ("qc", 384), ("kc", 384), ("vc", 384)):
    _PROJ[_n] = (_off, _w)
    _off += _w
N_PROJ = _off


def _bucket_thresholds():
    max_exact = NUM_BUCKETS // 2
    n = np.arange(max_exact, 4 * MAX_DISTANCE, dtype=np.float32)
    large = max_exact + (np.log(n / max_exact) / math.log(MAX_DISTANCE / max_exact)
                         * (NUM_BUCKETS - max_exact)).astype(np.int32)
    large = np.minimum(large, NUM_BUCKETS - 1)
    return [int(n[np.argmax(large >= b)]) for b in range(max_exact + 1, NUM_BUCKETS)]


_THRS = _bucket_thresholds()


def _mm(a, b):
    return lax.dot_general(a.astype(BF16), b.astype(BF16), (((1,), (0,)), ((), ())),
                           preferred_element_type=F32)


def _mm_nt(a, b):
    return lax.dot_general(a.astype(BF16), b.astype(BF16), (((1,), (1,)), ((), ())),
                           preferred_element_type=F32)


def _split(a):
    hi = a.astype(BF16)
    lo = (a - hi.astype(F32)).astype(BF16)
    return hi, lo


def _mm3(a, b):
    ah, al = _split(a)
    bh, bl = _split(b)
    return _mm(ah, bh) + (_mm(ah, bl) + _mm(al, bh))


def _sigmoid(x):
    return 1.0 / (1.0 + jnp.exp(-x))


def _rmsnorm(x, g):
    return x * lax.rsqrt(jnp.mean(x * x, axis=-1, keepdims=True) + EPS) * g


def _lane(shape=(1, LANES)):
    return lax.broadcasted_iota(I32, shape, len(shape) - 1)


def _bd(k, parts):
    grp = _lane() >> int(math.log2(LANES // parts))
    zero = jnp.zeros_like(k)
    return jnp.concatenate([jnp.where(grp == r, k, zero) for r in range(parts)], axis=0)


def _sel2(lo_val, hi_val):
    return jnp.where(_lane() < HEAD_DIM, lo_val, hi_val)


def _softmax_step(s, m, l):
    m_new = jnp.maximum(m, jnp.max(s, axis=-1, keepdims=True))
    p = jnp.exp(s - m_new)
    alpha = jnp.exp(m - m_new)
    l_new = alpha * l + jnp.sum(p, axis=-1, keepdims=True)
    return p, m_new, l_new, alpha


def _pair_softmax_tile(q, k, v, add, state):
    m_lo, m_hi, l_lo, l_hi, acc = state
    tk = k.shape[0]
    s = _mm_nt(q, _bd(k, 2)) + add
    p_lo, m_lo, l_lo, a_lo = _softmax_step(s[:, :tk], m_lo, l_lo)
    p_hi, m_hi, l_hi, a_hi = _softmax_step(s[:, tk:], m_hi, l_hi)
    pv = _mm(jnp.concatenate([p_lo, p_hi], axis=1), _bd(v, 2))
    acc = acc * _sel2(a_lo, a_hi) + pv
    return m_lo, m_hi, l_lo, l_hi, acc


def _pair_softmax_init(tq):
    col = jnp.full((tq, 1), M_INIT, F32)
    z = jnp.zeros((tq, 1), F32)
    return col, col, z, z, jnp.zeros((tq, LANES), F32)


def _pair_softmax_out(state):
    _, _, l_lo, l_hi, acc = state
    return acc * _sel2(1.0 / l_lo, 1.0 / l_hi)


def _diff_tile(q, k, v, add, state):
    ms, ls, acc1, acc2 = state
    tk = k.shape[0]
    s = _mm_nt(q, _bd(k, 4)) + add
    ps, ms2, ls2, al = [], [], [], []
    for r in range(4):
        p, m, l, a = _softmax_step(s[:, r * tk:(r + 1) * tk], ms[r], ls[r])
        ps.append(p); ms2.append(m); ls2.append(l); al.append(a)
    vbd = _bd(v, 2)
    acc1 = acc1 * _sel2(al[0], al[2]) + _mm(jnp.concatenate([ps[0], ps[2]], axis=1), vbd)
    acc2 = acc2 * _sel2(al[1], al[3]) + _mm(jnp.concatenate([ps[1], ps[3]], axis=1), vbd)
    return tuple(ms2), tuple(ls2), acc1, acc2


def _diff_init(tq):
    col = jnp.full((tq, 1), M_INIT, F32)
    z = jnp.zeros((tq, 1), F32)
    acc = jnp.zeros((tq, LANES), F32)
    return (col,) * 4, (z,) * 4, acc, acc


def _diff_out(state, lam, g2, out_scale):
    _, ls, acc1, acc2 = state
    o = acc1 * _sel2(1.0 / ls[0], 1.0 / ls[2]) - lam * (acc2 * _sel2(1.0 / ls[1], 1.0 / ls[3]))
    o2 = o * o
    lo = _lane() < HEAD_DIM
    ms_lo = jnp.sum(jnp.where(lo, o2, 0.0), axis=-1, keepdims=True) * (1.0 / HEAD_DIM)
    ms_hi = jnp.sum(jnp.where(lo, 0.0, o2), axis=-1, keepdims=True) * (1.0 / HEAD_DIM)
    r = _sel2(lax.rsqrt(ms_lo + EPS), lax.rsqrt(ms_hi + EPS))
    return o * r * g2 * out_scale


def _later_matrix():
    j = lax.broadcasted_iota(I32, (BLK, BLK), 0)
    s = lax.broadcasted_iota(I32, (BLK, BLK), 1)
    return (j > s).astype(BF16)


def _stick_tile(q, k, v, strict, later_mat, state):
    c_lo, c_hi, acc = state
    tk = k.shape[0]
    z = _mm_nt(q, _bd(k, 2))
    outs, carries = [], []
    for zz, c in ((z[:, :tk], c_lo), (z[:, tk:], c_hi)):
        lb = jnp.minimum(zz, 0.0) - jnp.log(1.0 + jnp.exp(-jnp.abs(zz)))
        lk = lb - zz
        if strict is not None:
            lk = jnp.where(strict, lk, 0.0)
        hi, lo = _split(lk)
        later = _mm(hi, later_mat) + _mm(lo, later_mat) + c
        a = jnp.exp(lb + later)
        if strict is not None:
            a = jnp.where(strict, a, 0.0)
        outs.append(a)
        carries.append(c + jnp.sum(lk, axis=-1, keepdims=True))
    acc = acc + _mm(jnp.concatenate(outs, axis=1), _bd(v, 2))
    return carries[0], carries[1], acc


def _stick_init(tq):
    z = jnp.zeros((tq, 1), F32)
    return z, z, jnp.zeros((tq, LANES), F32)


def _sortable(score):
    i = lax.bitcast_convert_type(score + 0.0, I32)
    return i ^ ((i >> 31) & 0x7FFFFFFF)


def _index_scores(q_tiles, kk, wi):
    tk = kk.shape[0]
    kbd = _bd(kk, 2)
    score = None
    for blk, qt in enumerate(q_tiles):
        s = _mm_nt(qt, kbd)
        for half in range(2):
            n = 2 * blk + half
            term = jnp.maximum(s[:, half * tk:(half + 1) * tk], 0.0) * wi[:, n:n + 1]
            score = term if score is None else score + term
    return score


def _stacked_index_scores(q_stack, kk, wi, tq):
    s = jnp.maximum(_mm_nt(q_stack, kk), 0.0)
    score = None
    for n in range(N_IDX_HEADS):
        term = s[n * tq:(n + 1) * tq] * wi[:, n:n + 1]
        score = term if score is None else score + term
    return score


def _topk_select(keys_ref, mask_ref, nblk, topk, tq, valid_fn):
    kf = float(topk)

    def count(pred):
        def body(kb, acc):
            return acc + pred(keys_ref[kb]).astype(F32)
        acc = lax.fori_loop(0, nblk, body, jnp.zeros((tq, LANES), F32))
        return jnp.sum(acc, axis=-1, keepdims=True)

    zero = jnp.zeros((tq, 1), I32)
    v = jnp.where(count(lambda key: key >= zero) >= kf, zero, jnp.full((tq, 1), INT_MIN, I32))

    def bit_body(bi, v):
        cand = v | lax.shift_left(jnp.int32(1), 30 - bi)
        return jnp.where(count(lambda key: key >= cand) >= kf, cand, v)

    v = lax.fori_loop(0, 31, bit_body, v)
    need = kf - count(lambda key: key > v)

    jj = lax.broadcasted_iota(I32, (BLK, BLK), 0)
    ss = lax.broadcasted_iota(I32, (BLK, BLK), 1)
    before = (jj < ss).astype(BF16)

    def sel_body(kb, run):
        key = keys_ref[kb]
        eqf = (key == v).astype(F32)
        rank = _mm(eqf, before) + run
        sel = (key > v) | ((key == v) & (rank < need))
        mask_ref[kb] = jnp.where(sel & valid_fn(kb), 0.0, NEG)
        return run + jnp.sum(eqf, axis=-1, keepdims=True)

    lax.fori_loop(0, nblk, sel_body, jnp.zeros((tq, 1), F32))


def _mod_kernel(c_ref, w_ref, b_ref, o_ref):
    c = c_ref[...]
    o_ref[...] = _mm3(c * _sigmoid(c), w_ref[...]) + b_ref[...]


def _modulation(c_all, w_mod, b_mod):
    depth, d, _ = w_mod.shape
    n = c_all.shape[0]
    return pl.pallas_call(
        _mod_kernel,
        grid=(depth, 6),
        in_specs=[pl.BlockSpec((n, d), lambda l, c: (0, 0)),
                  pl.BlockSpec((None, d, d), lambda l, c: (l, 0, c)),
                  pl.BlockSpec((None, None, 1, d), lambda l, c: (l, c, 0, 0))],
        out_specs=pl.BlockSpec((None, None, n, d), lambda l, c: (l, c, 0, 0)),
        out_shape=jax.ShapeDtypeStruct((depth, 6, n, d), F32),
        compiler_params=pltpu.CompilerParams(dimension_semantics=("arbitrary", "arbitrary"),
                                             vmem_limit_bytes=VMEM_LIMIT),
        name="modulation",
    )(c_all, w_mod, b_mod.reshape(depth, 6, 1, d))


def _bias_kernel(rb_ref, o_ref):
    h = pl.program_id(0)
    t = lax.broadcasted_iota(I32, (BLK, BLK), 0)
    s = lax.broadcasted_iota(I32, (BLK, BLK), 1)
    for ty, c in ((0, 0), (1, BLK)):
        rel = c + t - s
        n = jnp.maximum(rel, 0)
        large = jnp.full((BLK, BLK), NUM_BUCKETS // 2, I32)
        for thr in _THRS:
            large = large + (n >= thr).astype(I32)
        bucket = jnp.where(n < NUM_BUCKETS // 2, n, large)
        val = jnp.zeros((BLK, BLK), F32)
        for b in range(NUM_BUCKETS):
            val = jnp.where(bucket == b, rb_ref[b, h], val)
        if ty == 0:
            val = jnp.where(rel >= 0, val, NEG)
        o_ref[ty] = val
    o_ref[2] = jnp.full((BLK, BLK), rb_ref[NUM_BUCKETS - 1, h], F32)


def _bias_tiles(rel_bias):
    nh = rel_bias.shape[1]
    return pl.pallas_call(
        _bias_kernel,
        grid=(nh,),
        in_specs=[pl.BlockSpec(memory_space=pltpu.SMEM)],
        out_specs=pl.BlockSpec((None, 3, BLK, BLK), lambda h: (h, 0, 0, 0)),
        out_shape=jax.ShapeDtypeStruct((nh, 3, BLK, BLK), F32),
        name="bias_tiles",
    )(rel_bias)


_IN_OUTS = (("ka", F32), ("va", F32), ("ki", F32), ("kd", F32), ("vd", F32), ("kc", F32), ("vc", F32),
            ("qa", BF16), ("ka", BF16), ("va", BF16), ("qi", BF16), ("wi", F32), ("ki2", BF16),
            ("qd", BF16), ("kd", BF16), ("vd", BF16), ("qc", BF16), ("kc", BF16), ("vc", BF16))


def _in_kernel(x_ref, g_ref, sh_ref, sc_ref, w_ref, *outs):
    x = x_ref[...]
    h = _rmsnorm(x, g_ref[...]) * (1.0 + sc_ref[...]) + sh_ref[...]
    y = _mm(h, w_ref[...])
    for (name, dt), o in zip(_IN_OUTS, outs):
        off, w = _PROJ["ki2" if name == "ki" else name]
        if name == "ki":
            w = IDX_DIM
        o[...] = y[:, off:off + w].astype(dt)


def _mod_spec(tm, d, rows_per_mod, l, c):
    if rows_per_mod is None:
        return pl.BlockSpec((None, None, None, tm, d), lambda i: (l, c, 0, i, 0))
    return pl.BlockSpec((None, None, None, 1, d), lambda i: (l, c, i // rows_per_mod, 0, 0))


def _in_proj(x, g, mod, w_cat, l, tm, tiles_per_mod):
    m, d = x.shape
    out_shapes, out_specs = [], []
    for name, dt in _IN_OUTS:
        w = IDX_DIM if name == "ki" else _PROJ[name][1]
        out_shapes.append(jax.ShapeDtypeStruct((m, w), dt))
        out_specs.append(pl.BlockSpec((tm, w), lambda i: (i, 0)))
    return pl.pallas_call(
        _in_kernel,
        grid=(m // tm,),
        in_specs=[pl.BlockSpec((tm, d), lambda i: (i, 0)),
                  pl.BlockSpec((None, 1, d), lambda i: (l, 0, 0)),
                  _mod_spec(tm, d, tiles_per_mod, l, 0),
                  _mod_spec(tm, d, tiles_per_mod, l, 1),
                  pl.BlockSpec((None, d, N_PROJ), lambda i: (l, 0, 0))],
        out_specs=out_specs,
        out_shape=out_shapes,
        compiler_params=pltpu.CompilerParams(dimension_semantics=("arbitrary",),
                                             vmem_limit_bytes=VMEM_LIMIT),
        name="in_proj",
    )(x, g, mod, mod, w_cat)


def _out_kernel(mix_ref, x_ref, gt_ref, w_ref, o_ref):
    o_ref[...] = x_ref[...] + gt_ref[...] * _mm(mix_ref[...], w_ref[...])


def _out_proj(mix, x, mod, w_out, l, tm, tiles_per_mod):
    m, d = x.shape
    return pl.pallas_call(
        _out_kernel,
        grid=(m // tm,),
        in_specs=[pl.BlockSpec((tm, d), lambda i: (i, 0)),
                  pl.BlockSpec((tm, d), lambda i: (i, 0)),
                  _mod_spec(tm, d, tiles_per_mod, l, 2),
                  pl.BlockSpec((None, d, d), lambda i: (l, 0, 0))],
        out_specs=pl.BlockSpec((tm, d), lambda i: (i, 0)),
        out_shape=jax.ShapeDtypeStruct((m, d), F32),
        compiler_params=pltpu.CompilerParams(dimension_semantics=("arbitrary",),
                                             vmem_limit_bytes=VMEM_LIMIT),
        name="out_proj",
    )(mix, x, mod, w_out)


def _ffn_kernel(x_ref, g_ref, sh_ref, sc_ref, gt_ref, wg_ref, wu_ref, wo_ref, gf_ref, o_ref,
                h_ref, acc_ref, *, final):
    j = pl.program_id(1)

    @pl.when(j == 0)
    def _():
        h = _rmsnorm(x_ref[...], g_ref[...]) * (1.0 + sc_ref[...]) + sh_ref[...]
        h_ref[...] = h.astype(BF16)
        acc_ref[...] = jnp.zeros_like(acc_ref)

    h = h_ref[...]
    gate = _mm(h, wg_ref[...])
    up = _mm(h, wu_ref[...])
    acc_ref[...] += _mm(gate * _sigmoid(gate) * up, wo_ref[...])

    @pl.when(j == pl.num_programs(1) - 1)
    def _():
        y = x_ref[...] + gt_ref[...] * acc_ref[...]
        if final:
            y = _rmsnorm(y, gf_ref[...])
        o_ref[...] = y


def _ffn(x, g, mod, w_in, w_out, g_final, l, tm, tiles_per_mod, final):
    m, d = x.shape
    dff = w_out.shape[1]
    nf = 2 if dff % (2 * LANES) == 0 else 1
    tf = dff // nf

    def ms(c):
        spec = _mod_spec(tm, d, tiles_per_mod, l, c)
        return pl.BlockSpec(spec.block_shape, lambda i, j, f=spec.index_map: f(i))

    return pl.pallas_call(
        functools.partial(_ffn_kernel, final=final),
        grid=(m // tm, nf),
        in_specs=[pl.BlockSpec((tm, d), lambda i, j: (i, 0)),
                  pl.BlockSpec((None, 1, d), lambda i, j: (l, 0, 0)),
                  ms(3), ms(4), ms(5),
                  pl.BlockSpec((None, d, tf), lambda i, j: (l, 0, j)),
                  pl.BlockSpec((None, d, tf), lambda i, j: (l, 0, nf + j)),
                  pl.BlockSpec((None, tf, d), lambda i, j: (l, j, 0)),
                  pl.BlockSpec((1, d), lambda i, j: (0, 0))],
        out_specs=pl.BlockSpec((tm, d), lambda i, j: (i, 0)),
        out_shape=jax.ShapeDtypeStruct((m, d), F32),
        scratch_shapes=[pltpu.VMEM((tm, d), BF16), pltpu.VMEM((tm, d), F32)],
        compiler_params=pltpu.CompilerParams(dimension_semantics=("arbitrary", "arbitrary"),
                                             vmem_limit_bytes=VMEM_LIMIT),
        name="ffn",
    )(x, g, mod, mod, mod, w_in, w_in, w_out, g_final)


def _lambda(dl_ref, lam_init):
    dl = dl_ref[...]
    a = jnp.sum(dl[0:1] * dl[1:2], axis=(0, 1), keepdims=True)
    b = jnp.sum(dl[2:3] * dl[3:4], axis=(0, 1), keepdims=True)
    return jnp.exp(a) - jnp.exp(b) + lam_init


def _attn_p_kernel(qa_ref, ka_ref, va_ref, qi_ref, wi_ref, ki2_ref, qd_ref, kd_ref, vd_ref,
                   qc_ref, kc_ref, vc_ref, ta_ref, tb_ref, dl_ref, g2_ref, o_ref,
                   keys_ref, mask_ref, *, topk, lam_init):
    i = pl.program_id(1)
    tq = BLK
    nblk = i + 1
    t_loc = lax.broadcasted_iota(I32, (tq, BLK), 0)
    s_loc = lax.broadcasted_iota(I32, (tq, BLK), 1)

    def rows(kb):
        return pl.ds(pl.multiple_of(kb * BLK, BLK), BLK)

    def causal(kb):
        return kb * BLK + s_loc <= i * BLK + t_loc

    qi_tiles = [qi_ref[:, b * LANES:(b + 1) * LANES] for b in range(2)]
    wi = wi_ref[...]

    def score_body(kb, carry):
        score = _index_scores(qi_tiles, ki2_ref[rows(kb), :], wi)
        keys_ref[kb] = jnp.where(causal(kb), _sortable(score), INT_MIN)
        return carry

    lax.fori_loop(0, nblk, score_body, 0)
    _topk_select(keys_ref, mask_ref, nblk, topk, tq, causal)

    def bias_type(kb):
        return jnp.minimum(i - kb, 2)

    for p in range(H_A // 2):
        cols = slice(p * LANES, (p + 1) * LANES)
        q = qa_ref[:, cols]

        def body(kb, state, cols=cols, q=q, p=p):
            am = mask_ref[kb]
            add = ta_ref[p, bias_type(kb)] + jnp.concatenate([am, am], axis=1)
            return _pair_softmax_tile(q, ka_ref[rows(kb), cols], va_ref[rows(kb), cols], add, state)

        state = lax.fori_loop(0, nblk, body, _pair_softmax_init(tq))
        o_ref[:, cols] = _pair_softmax_out(state).astype(o_ref.dtype)

    lam = _lambda(dl_ref, lam_init)
    g2 = g2_ref[...]
    base = H_A * HEAD_DIM
    for u in range(H_B // 2):
        cols = slice(u * LANES, (u + 1) * LANES)
        q = qd_ref[:, cols]

        def body(kb, state, cols=cols, q=q, u=u):
            return _diff_tile(q, kd_ref[rows(kb), cols], vd_ref[rows(kb), cols],
                              tb_ref[u, bias_type(kb)], state)

        state = lax.fori_loop(0, nblk, body, _diff_init(tq))
        o_ref[:, base + u * LANES:base + (u + 1) * LANES] = _diff_out(
            state, lam, g2, 1.0 - lam_init).astype(o_ref.dtype)

    later_mat = _later_matrix()
    base = (H_A + H_B) * HEAD_DIM
    for p in range(H_C // 2):
        cols = slice(p * LANES, (p + 1) * LANES)
        q = qc_ref[:, cols]

        def body(r, state, cols=cols, q=q):
            kb = i - r
            strict = kb * BLK + s_loc < i * BLK + t_loc
            return _stick_tile(q, kc_ref[rows(kb), cols], vc_ref[rows(kb), cols], strict,
                               later_mat, state)

        state = lax.fori_loop(0, nblk, body, _stick_init(tq))
        o_ref[:, base + p * LANES:base + (p + 1) * LANES] = state[2].astype(o_ref.dtype)


def _attn_prompt(pr, ta, tb, dl, g2, l, b, t, topk, lam_init):
    nq = t // BLK

    def qspec(w):
        return pl.BlockSpec((None, BLK, w), lambda bb, i: (bb, i, 0))

    def kspec(w):
        return pl.BlockSpec((None, t, w), lambda bb, i: (bb, 0, 0))

    def full(a):
        return pl.BlockSpec(a.shape, lambda bb, i, n=a.ndim: (0,) * n)

    r3 = lambda a: a.reshape(b, t, a.shape[-1])
    args = [r3(pr["qa"]), r3(pr["ka_b"]), r3(pr["va_b"]), r3(pr["qi"]), r3(pr["wi"]), r3(pr["ki2"]),
            r3(pr["qd"]), r3(pr["kd_b"]), r3(pr["vd_b"]), r3(pr["qc"]), r3(pr["kc_b"]), r3(pr["vc_b"])]
    specs = [qspec(384), kspec(384), kspec(384), qspec(256), qspec(128), kspec(128),
             qspec(256), kspec(256), kspec(256), qspec(384), kspec(384), kspec(384)]
    return pl.pallas_call(
        functools.partial(_attn_p_kernel, topk=topk, lam_init=lam_init),
        grid=(b, nq),
        in_specs=specs + [full(ta), full(tb),
                          pl.BlockSpec((None, 4, HD_B), lambda bb, i: (l, 0, 0)),
                          pl.BlockSpec((None, 1, LANES), lambda bb, i: (l, 0, 0))],
        out_specs=pl.BlockSpec((None, BLK, 1024), lambda bb, i: (bb, i, 0)),
        out_shape=jax.ShapeDtypeStruct((b, t, 1024), BF16),
        scratch_shapes=[pltpu.VMEM((nq, BLK, BLK), I32), pltpu.VMEM((nq, BLK, BLK), F32)],
        compiler_params=pltpu.CompilerParams(dimension_semantics=("arbitrary", "arbitrary"),
                                             vmem_limit_bytes=VMEM_LIMIT),
        name="attn_prompt",
    )(*args, ta, tb, dl, g2)


def _sel_s_kernel(pt_ref, qi_ref, wi_ref, kn_ref, *rest, g, n_pages, topk):
    page_refs = rest[:g]
    o_ref = rest[g]
    keys_ref, mask_ref = rest[g + 1], rest[g + 2]
    j = pl.program_id(1)
    q = qi_ref[...]
    wi = wi_ref[...]
    for r in range(g):
        score = _stacked_index_scores(q, page_refs[r][...], wi, SQ)
        keys_ref[j * g + r] = _sortable(score)

    @pl.when(j == pl.num_programs(1) - 1)
    def _():
        t_loc = lax.broadcasted_iota(I32, (SQ, BLK), 0)
        s_loc = lax.broadcasted_iota(I32, (SQ, BLK), 1)
        new_ok = s_loc <= t_loc
        score = _stacked_index_scores(q, kn_ref[...], wi, SQ)
        keys_ref[n_pages] = jnp.where(new_ok, _sortable(score), INT_MIN)
        _topk_select(keys_ref, mask_ref, n_pages + 1, topk, SQ,
                     lambda kb: (kb < n_pages) | new_ok)
        for kb in range(n_pages + 1):
            o_ref[:, kb * BLK:(kb + 1) * BLK] = mask_ref[kb]


def _select_sample(page_table, qi_stack, wi, ki_new, pool_kidx, l, g, topk):
    nb, n_pages = page_table.shape
    ns = n_pages // g

    def page_spec(r):
        return pl.BlockSpec((None, None, BLK, IDX_DIM),
                            lambda bb, j, pt, r=r: (l, pt[bb, j * g + r], 0, 0))

    grid_spec = pltpu.PrefetchScalarGridSpec(
        num_scalar_prefetch=1,
        grid=(nb, ns),
        in_specs=[pl.BlockSpec((None, N_IDX_HEADS * SQ, IDX_DIM), lambda bb, j, pt: (bb, 0, 0)),
                  pl.BlockSpec((None, SQ, LANES), lambda bb, j, pt: (bb, 0, 0)),
                  pl.BlockSpec((None, BLK, IDX_DIM), lambda bb, j, pt: (bb, 0, 0))]
                 + [page_spec(r) for r in range(g)],
        out_specs=pl.BlockSpec((None, SQ, (n_pages + 1) * BLK), lambda bb, j, pt: (bb, 0, 0)),
        scratch_shapes=[pltpu.VMEM((n_pages + 1, SQ, BLK), I32),
                        pltpu.VMEM((n_pages + 1, SQ, BLK), F32)],
    )
    return pl.pallas_call(
        functools.partial(_sel_s_kernel, g=g, n_pages=n_pages, topk=topk),
        grid_spec=grid_spec,
        out_shape=jax.ShapeDtypeStruct((nb, SQ, (n_pages + 1) * BLK), F32),
        compiler_params=pltpu.CompilerParams(dimension_semantics=("arbitrary", "arbitrary"),
                                             vmem_limit_bytes=VMEM_LIMIT),
        name="select_sample",
    )(page_table, qi_stack, wi, ki_new, *([pool_kidx] * g))


def _attn_s_kernel(pt_ref, qa_ref, qd_ref, qc_ref, kan_ref, van_ref, kdn_ref, vdn_ref, kcn_ref,
                   vcn_ref, mnew_ref, mask_ref, ta_ref, tb_ref, dl_ref, g2_ref, *rest,
                   g, n_pages, lam_init):
    pools = [rest[k * g:(k + 1) * g] for k in range(6)]
    o_ref = rest[6 * g]
    sa_ref, sb_ref, sc_ref = rest[6 * g + 1:6 * g + 4]
    j = pl.program_id(1)
    ns = pl.num_programs(1)
    tq = SQ
    later_mat = _later_matrix()
    t_loc = lax.broadcasted_iota(I32, (tq, BLK), 0)
    s_loc = lax.broadcasted_iota(I32, (tq, BLK), 1)

    def col(x):
        return x[:, 0:1]

    def wide(x):
        return jnp.broadcast_to(x, (tq, LANES))

    def load_a(p):
        return (col(sa_ref[p, 0]), col(sa_ref[p, 1]), col(sa_ref[p, 2]), col(sa_ref[p, 3]), sa_ref[p, 4])

    def store_a(p, st):
        for k in range(4):
            sa_ref[p, k] = wide(st[k])
        sa_ref[p, 4] = st[4]

    def load_b(u):
        return (tuple(col(sb_ref[u, k]) for k in range(4)),
                tuple(col(sb_ref[u, 4 + k]) for k in range(4)), sb_ref[u, 8], sb_ref[u, 9])

    def store_b(u, st):
        for k in range(4):
            sb_ref[u, k] = wide(st[0][k])
            sb_ref[u, 4 + k] = wide(st[1][k])
        sb_ref[u, 8] = st[2]
        sb_ref[u, 9] = st[3]

    def load_c(p):
        return col(sc_ref[p, 0]), col(sc_ref[p, 1]), sc_ref[p, 2]

    def store_c(p, st):
        sc_ref[p, 0] = wide(st[0])
        sc_ref[p, 1] = wide(st[1])
        sc_ref[p, 2] = st[2]

    def visit(get_kv, bias_ty, am, strict, states):
        sa, sb, sc = states
        sa2, sb2, sc2 = [], [], []
        for p in range(H_A // 2):
            cols = slice(p * LANES, (p + 1) * LANES)
            add = ta_ref[p, bias_ty, 0:tq, :] + jnp.concatenate([am, am], axis=1)
            sa2.append(_pair_softmax_tile(qa_ref[:, cols], get_kv(0)(cols), get_kv(1)(cols), add, sa[p]))
        for u in range(H_B // 2):
            cols = slice(u * LANES, (u + 1) * LANES)
            sb2.append(_diff_tile(qd_ref[:, cols], get_kv(2)(cols), get_kv(3)(cols),
                                  tb_ref[u, bias_ty, 0:tq, :], sb[u]))
        for p in range(H_C // 2):
            cols = slice(p * LANES, (p + 1) * LANES)
            sc2.append(_stick_tile(qc_ref[:, cols], get_kv(4)(cols), get_kv(5)(cols), strict,
                                   later_mat, sc[p]))
        return sa2, sb2, sc2

    @pl.when(j == 0)
    def _():
        states = ([_pair_softmax_init(tq)] * (H_A // 2), [_diff_init(tq)] * (H_B // 2),
                  [_stick_init(tq)] * (H_C // 2))
        new_refs = (kan_ref, van_ref, kdn_ref, vdn_ref, kcn_ref, vcn_ref)
        sa, sb, sc = visit(lambda kind: (lambda cols: new_refs[kind][:, cols].astype(BF16)), 0, mnew_ref[...],
                           s_loc < t_loc, states)
        for p in range(H_A // 2):
            store_a(p, sa[p])
        for u in range(H_B // 2):
            store_b(u, sb[u])
        for p in range(H_C // 2):
            store_c(p, sc[p])

    states = ([load_a(p) for p in range(H_A // 2)], [load_b(u) for u in range(H_B // 2)],
              [load_c(p) for p in range(H_C // 2)])
    first_page = n_pages - (j + 1) * g
    for r in range(g - 1, -1, -1):
        page = first_page + r
        bias_ty = jnp.where(page == n_pages - 1, 1, 2)
        states = visit(lambda kind, r=r: (lambda cols: pools[kind][r][:, cols].astype(BF16)), bias_ty,
                       mask_ref[:, r * BLK:(r + 1) * BLK], None, states)
    sa, sb, sc = states
    for p in range(H_A // 2):
        store_a(p, sa[p])
    for u in range(H_B // 2):
        store_b(u, sb[u])
    for p in range(H_C // 2):
        store_c(p, sc[p])

    @pl.when(j == ns - 1)
    def _():
        lam = _lambda(dl_ref, lam_init)
        g2 = g2_ref[...]
        for p in range(H_A // 2):
            o_ref[:, p * LANES:(p + 1) * LANES] = _pair_softmax_out(sa[p]).astype(o_ref.dtype)
        base = H_A * HEAD_DIM
        for u in range(H_B // 2):
            o_ref[:, base + u * LANES:base + (u + 1) * LANES] = _diff_out(
                sb[u], lam, g2, 1.0 - lam_init).astype(o_ref.dtype)
        base = (H_A + H_B) * HEAD_DIM
        for p in range(H_C // 2):
            o_ref[:, base + p * LANES:base + (p + 1) * LANES] = sc[p][2].astype(o_ref.dtype)


def _attn_sample(page_table, q_arrays, new_arrays, addmask, ta, tb, dl, g2, pools, l, g, lam_init):
    nb, n_pages = page_table.shape
    ns = n_pages // g

    def seq_spec(a):
        return pl.BlockSpec((None,) + a.shape[1:], lambda bb, j, pt: (bb, 0, 0))

    def full(a):
        return pl.BlockSpec(a.shape, lambda bb, j, pt, n=a.ndim: (0,) * n)

    def page_spec(w, r):
        return pl.BlockSpec((None, None, BLK, w),
                            lambda bb, j, pt, r=r: (l, pt[bb, n_pages - (j + 1) * g + r], 0, 0))

    in_specs = [seq_spec(a) for a in q_arrays] + [seq_spec(a) for a in new_arrays]
    in_specs += [pl.BlockSpec((None, SQ, BLK), lambda bb, j, pt: (bb, 0, n_pages)),
                 pl.BlockSpec((None, SQ, g * BLK), lambda bb, j, pt: (bb, 0, ns - 1 - j)),
                 full(ta), full(tb),
                 pl.BlockSpec((None, 4, HD_B), lambda bb, j, pt: (l, 0, 0)),
                 pl.BlockSpec((None, 1, LANES), lambda bb, j, pt: (l, 0, 0))]
    args = list(q_arrays) + list(new_arrays) + [addmask, addmask, ta, tb, dl, g2]
    for pool in pools:
        w = pool.shape[-1]
        in_specs += [page_spec(w, r) for r in range(g)]
        args += [pool] * g
    grid_spec = pltpu.PrefetchScalarGridSpec(
        num_scalar_prefetch=1,
        grid=(nb, ns),
        in_specs=in_specs,
        out_specs=pl.BlockSpec((None, SQ, 1024), lambda bb, j, pt: (bb, 0, 0)),
        scratch_shapes=[pltpu.VMEM((H_A // 2, 5, SQ, LANES), F32),
                        pltpu.VMEM((H_B // 2, 10, SQ, LANES), F32),
                        pltpu.VMEM((H_C // 2, 3, SQ, LANES), F32)],
    )
    return pl.pallas_call(
        functools.partial(_attn_s_kernel, g=g, n_pages=n_pages, lam_init=lam_init),
        grid_spec=grid_spec,
        out_shape=jax.ShapeDtypeStruct((nb, SQ, 1024), BF16),
        compiler_params=pltpu.CompilerParams(dimension_semantics=("arbitrary", "arbitrary"),
                                             vmem_limit_bytes=VMEM_LIMIT),
        name="attn_sample",
    )(page_table, *args)


def _cat_weights(w_in):
    splits = (384, 384, 384, 256, 4, 64, 256, 256, 256, 384, 384, 384)
    offs = np.concatenate([[0], np.cumsum(splits)])
    part = {n: w_in[:, :, offs[k]:offs[k + 1]] for k, n in enumerate(
        ("qa", "ka", "va", "qi", "wi", "ki", "qd", "kd", "vd", "qc", "kc", "vc"))}
    part["qa"] = part["qa"] * HEAD_DIM ** -0.5
    part["qc"] = part["qc"] * HEAD_DIM ** -0.5
    part["qi"] = part["qi"] * IDX_DIM ** -0.5
    part["qd"] = part["qd"] * HD_B ** -0.5
    part["wi"] = jnp.pad(part["wi"] * N_IDX_HEADS ** -0.5, ((0, 0), (0, 0), (0, LANES - N_IDX_HEADS)))
    part["ki2"] = jnp.concatenate([part["ki"], part["ki"]], axis=-1)
    return jnp.concatenate([part[n] for n in _PROJ], axis=-1).astype(BF16)


def _row_tile(m):
    for tm in (256, 128, 64, 32, 16, 8):
        if m % tm == 0:
            return tm
    raise ValueError(m)


def kernel(x_prompt, x_sample, c_prompt, c_sample, cache_a_k, cache_a_v, cache_a_kidx, cache_b_k,
           cache_b_v, cache_c_k, cache_c_v, page_table, rel_bias, w_mod, b_mod, g_attn, g_ffn, w_in,
           diff_lambda, subln_g, w_out, w_ffn_in, w_ffn_out, g_final):
    b, t, d = x_prompt.shape
    nb, ts, _ = x_sample.shape
    depth = w_in.shape[0]
    n_pool, page = cache_a_k.shape[1], cache_a_k.shape[2]
    n_pages = page_table.shape[1]
    assert page == BLK and t % BLK == 0 and ts <= 8 and d == 1024
    g_pages = 8 if n_pages % 8 == 0 else (4 if n_pages % 4 == 0 else 1)
    topk_p = min(TOPK_MAX, t // 4)
    topk_s = min(TOPK_MAX, (n_pages * page + ts) // 4)

    w_cat = _cat_weights(w_in)
    w_out_b = w_out.astype(BF16)
    w_ffn_in_b = w_ffn_in.astype(BF16)
    w_ffn_out_b = w_ffn_out.astype(BF16)
    g_attn3 = g_attn.reshape(depth, 1, d)
    g_ffn3 = g_ffn.reshape(depth, 1, d)
    g_final2 = g_final.reshape(1, d)
    g2 = jnp.concatenate([subln_g, subln_g], axis=-1).reshape(depth, 1, LANES)
    pools = [cache_a_k.reshape(depth, n_pool, page, -1), cache_a_v.reshape(depth, n_pool, page, -1),
             cache_b_k.reshape(depth, n_pool, page, -1), cache_b_v.reshape(depth, n_pool, page, -1),
             cache_c_k.reshape(depth, n_pool, page, -1), cache_c_v.reshape(depth, n_pool, page, -1)]

    mod = _modulation(jnp.concatenate([c_prompt, c_sample], axis=0), w_mod, b_mod)
    mod_p = mod[:, :, :b].reshape(depth, 6, b, 1, d)
    mod_s = jnp.repeat(mod[:, :, b:], ts, axis=2).reshape(depth, 6, 1, nb * ts, d)

    tiles = _bias_tiles(rel_bias)
    ta = jnp.stack([jnp.concatenate([tiles[2 * p], tiles[2 * p + 1]], axis=-1)
                    for p in range(H_A // 2)])
    tb = jnp.stack([jnp.concatenate([tiles[H_A + 2 * u]] * 2 + [tiles[H_A + 2 * u + 1]] * 2, axis=-1)
                    for u in range(H_B // 2)])

    names = [n if dt == F32 or n in ("qa", "qi", "ki2", "qd", "qc") else n + "_b" for n, dt in _IN_OUTS]
    xp = x_prompt.reshape(b * t, d)
    xs = x_sample.reshape(nb * ts, d)
    tm_p = _row_tile(t)
    tm_s = nb * ts
    new_rows = {k: ([], []) for k in ("ka", "va", "ki", "kd", "vd", "kc", "vc")}

    def pad_q(a):
        return jnp.pad(a.reshape(nb, ts, -1), ((0, 0), (0, SQ - ts), (0, 0)))

    def pad_new(a):
        return jnp.pad(a.reshape(nb, ts, -1), ((0, 0), (0, BLK - ts), (0, 0)))

    for l in range(depth):
        lam_init = 0.8 - 0.6 * math.exp(-0.3 * l)
        final = l == depth - 1
        pr = dict(zip(names, _in_proj(xp, g_attn3, mod_p, w_cat, l, tm_p, t // tm_p)))
        mix = _attn_prompt(pr, ta, tb, diff_lambda, g2, l, b, t, topk_p, lam_init)
        xp = _out_proj(mix.reshape(b * t, d), xp, mod_p, w_out_b, l, tm_p, t // tm_p)
        xp = _ffn(xp, g_ffn3, mod_p, w_ffn_in_b, w_ffn_out_b, g_final2, l, tm_p, t // tm_p, final)
        sr = dict(zip(names, _in_proj(xs, g_attn3, mod_s, w_cat, l, tm_s, None)))
        qi_stack = pad_q(sr["qi"]).reshape(nb, SQ, N_IDX_HEADS, IDX_DIM).transpose(0, 2, 1, 3)
        qi_stack = qi_stack.reshape(nb, N_IDX_HEADS * SQ, IDX_DIM)
        addmask = _select_sample(page_table, qi_stack, pad_q(sr["wi"]), pad_new(sr["ki2"][:, :IDX_DIM]),
                                 cache_a_kidx, l, g_pages, topk_s)
        mix_s = _attn_sample(page_table,
                             [pad_q(sr["qa"]), pad_q(sr["qd"]), pad_q(sr["qc"])],
                             [pad_new(sr[n]) for n in ("ka_b", "va_b", "kd_b", "vd_b", "kc_b", "vc_b")],
                             addmask, ta, tb, diff_lambda, g2, pools, l, g_pages, lam_init)
        mix_s = mix_s[:, :ts].reshape(nb * ts, d)
        xs = _out_proj(mix_s, xs, mod_s, w_out_b, l, tm_s, None)
        xs = _ffn(xs, g_ffn3, mod_s, w_ffn_in_b, w_ffn_out_b, g_final2, l, tm_s, None, final)
        for k in new_rows:
            new_rows[k][0].append(pr[k])
            new_rows[k][1].append(sr[k])

    def stack(k, which, lead, tail):
        return jnp.stack(new_rows[k][which]).reshape((depth,) + lead + tail)

    tails = {"ka": (H_A, HEAD_DIM), "va": (H_A, HEAD_DIM), "ki": (IDX_DIM,), "kd": (H_B, 2, HD_B),
             "vd": (H_B, 2 * HD_B), "kc": (H_C, HEAD_DIM), "vc": (H_C, HEAD_DIM)}
    order = ("ka", "va", "ki", "kd", "vd", "kc", "vc")
    outs_p = [stack(k, 0, (b, t), tails[k]) for k in order]
    outs_s = [stack(k, 1, (nb, ts), tails[k]) for k in order]
    return (xp.reshape(b, t, d), xs.reshape(nb, ts, d), *outs_p, *outs_s)
```

```python
import functools
import math

import numpy as np
import jax
import jax.numpy as jnp
from jax import lax
from jax.experimental import pallas as pl
from jax.experimental.pallas import tpu as pltpu

F32, BF16, I32 = jnp.float32, jnp.bfloat16, jnp.int32

HEAD_DIM = 64
H_A, H_B, H_C = 6, 4, 6
HD_B = HEAD_DIM // 2
N_IDX_HEADS = 4
IDX_DIM = 64
TOPK_MAX = 256
NUM_BUCKETS = 32
MAX_DISTANCE = 128
EPS = 1e-6
NEG = -1e30
M_INIT = -3e38
INT_MIN = -2 ** 31
LANES = 128
BLK = 128
SQ = 16
KT_PROMPT = 4
VMEM_LIMIT = 48 * 1024 * 1024
NPA, NPB, NPC = H_A // 2, H_B // 2, H_C // 2


def _layout(groups):
    out, off = {}, 0
    for n, w in groups:
        out[n] = (off, w)
        off += w
    return out, off


_QCOLS, N_Q = _layout((("qa", 384), ("qi", 256), ("wi", 128), ("qd", 256), ("qc", 384)))
_KROWS, N_K = _layout((("ka", 384), ("va", 384), ("ki2", 128), ("kd", 256), ("vd", 256),
                       ("kc", 384), ("vc", 384)))
_KV = ("ka", "va", "ki", "kd", "vd", "kc", "vc")


def _krow(name):
    return (_KROWS["ki2"][0], IDX_DIM) if name == "ki" else _KROWS[name]


def _bucket_thresholds():
    max_exact = NUM_BUCKETS // 2
    n = np.arange(max_exact, 4 * MAX_DISTANCE, dtype=np.float32)
    large = max_exact + (np.log(n / max_exact) / math.log(MAX_DISTANCE / max_exact)
                         * (NUM_BUCKETS - max_exact)).astype(np.int32)
    large = np.minimum(large, NUM_BUCKETS - 1)
    return [int(n[np.argmax(large >= b)]) for b in range(max_exact + 1, NUM_BUCKETS)]


_THRS = _bucket_thresholds()


def _mm(a, b):
    return lax.dot_general(a.astype(BF16), b.astype(BF16), (((1,), (0,)), ((), ())),
                           preferred_element_type=F32)


def _mm_nt(a, b):
    return lax.dot_general(a.astype(BF16), b.astype(BF16), (((1,), (1,)), ((), ())),
                           preferred_element_type=F32)


def _split(a):
    hi = a.astype(BF16)
    lo = (a - hi.astype(F32)).astype(BF16)
    return hi, lo


def _mm3(a, b):
    ah, al = _split(a)
    bh, bl = _split(b)
    return _mm(ah, bh) + (_mm(ah, bl) + _mm(al, bh))


def _sigmoid(x):
    return 1.0 / (1.0 + jnp.exp(-x))


def _rmsnorm(x, g):
    return x * lax.rsqrt(jnp.mean(x * x, axis=-1, keepdims=True) + EPS) * g


def _lane():
    return lax.broadcasted_iota(I32, (1, LANES), 1)


def _stack_q(q, parts):
    grp = _lane() >> int(math.log2(LANES // parts))
    zero = jnp.zeros_like(q)
    return jnp.concatenate([jnp.where(grp == r, q, zero) for r in range(parts)], axis=0)


def _sel2(lo_val, hi_val):
    return jnp.where(_lane() < HEAD_DIM, lo_val, hi_val)


def _rowsum(x):
    return jnp.sum(x, axis=-1, keepdims=True)


def _tile_lanes(x, n):
    return x if n == 1 else jnp.concatenate([x] * n, axis=1)


def _softmax_update(m_ref, l_ref, acc_ref, idx, q_st, kk, vv, add):
    s = _mm(q_st, kk) + add
    m_old = m_ref[idx]
    m_new = jnp.maximum(m_old, jnp.max(s, axis=-1, keepdims=True))
    p = jnp.exp(s - _tile_lanes(m_new, s.shape[1] // LANES)).astype(BF16)
    alpha = jnp.exp(m_old - m_new)
    both = _mm_nt(p, jnp.concatenate([vv.astype(BF16), jnp.ones(vv.shape, BF16)], axis=0))
    l_ref[idx] = alpha * l_ref[idx] + both[:, LANES:]
    acc_ref[idx] = alpha * acc_ref[idx] + both[:, :LANES]
    m_ref[idx] = m_new


def _softmax_init(m_ref, l_ref, acc_ref):
    m_ref[...] = jnp.full(m_ref.shape, M_INIT, F32)
    l_ref[...] = jnp.zeros(l_ref.shape, F32)
    acc_ref[...] = jnp.zeros(acc_ref.shape, F32)


def _pair_out(acc, l, tq):
    o = acc / l
    return _sel2(o[:tq], o[tq:])


def _diff_out(acc, l, tq, lam, g2, out_scale):
    o = acc / l
    o = _sel2(o[:tq], o[2 * tq:3 * tq]) - lam * _sel2(o[tq:2 * tq], o[3 * tq:])
    o2 = o * o
    lo = _lane() < HEAD_DIM
    ms_lo = _rowsum(jnp.where(lo, o2, 0.0)) * (1.0 / HEAD_DIM)
    ms_hi = _rowsum(jnp.where(lo, 0.0, o2)) * (1.0 / HEAD_DIM)
    r = _sel2(lax.rsqrt(ms_lo + EPS), lax.rsqrt(ms_hi + EPS))
    return o * r * g2 * out_scale


def _later_matrix():
    j = lax.broadcasted_iota(I32, (BLK, 2 * BLK), 0)
    s = lax.broadcasted_iota(I32, (BLK, 2 * BLK), 1)
    return ((j > s) | (s >= BLK)).astype(BF16)


def _stick_update(c_ref, acc_ref, idx, q_st, kk, vv, strict, later_mat):
    n = kk.shape[1] // BLK
    z = _mm(q_st, kk)
    lb = jnp.minimum(z, 0.0) - jnp.log(1.0 + jnp.exp(-jnp.abs(z)))
    lk = lb - z
    if strict is not None:
        lk = jnp.where(strict, lk, 0.0)
    hi, lo = _split(lk)
    c = c_ref[idx]
    laters = [None] * n
    for r in reversed(range(n)):
        sl = slice(r * BLK, (r + 1) * BLK)
        both = _mm(hi[:, sl], later_mat) + _mm(lo[:, sl], later_mat)
        laters[r] = both[:, :BLK] + c
        c = c + both[:, BLK:]
    later = laters[0] if n == 1 else jnp.concatenate(laters, axis=1)
    a = jnp.exp(lb + later)
    if strict is not None:
        a = jnp.where(strict, a, 0.0)
    acc_ref[idx] = acc_ref[idx] + _mm_nt(a, vv)
    c_ref[idx] = c


def _sortable(score):
    i = lax.bitcast_convert_type(score + 0.0, I32)
    return i ^ ((i >> 31) & 0x7FFFFFFF)


def _before_matrix():
    jj = lax.broadcasted_iota(I32, (BLK, BLK), 0)
    ss = lax.broadcasted_iota(I32, (BLK, BLK), 1)
    return (jj < ss).astype(BF16)


def _kth_largest(count_ge, tq, topk):
    kf = float(topk)
    zero = jnp.zeros((tq, 1), I32)
    v = jnp.where(count_ge(zero) >= kf, zero, jnp.full((tq, 1), INT_MIN, I32))

    def bit_body(bi, v):
        cand = v | lax.shift_left(jnp.int32(1), 30 - bi)
        return jnp.where(count_ge(cand) >= kf, cand, v)

    return lax.fori_loop(0, 31, bit_body, v)


def _topk_select_blocks(keys_ref, mask_ref, nblk, topk, tq, valid_fn):
    def count(pred):
        def body(kb, acc):
            return acc + pred(keys_ref[kb]).astype(F32)
        return _rowsum(lax.fori_loop(0, nblk, body, jnp.zeros((tq, LANES), F32)))

    v = _kth_largest(lambda cand: count(lambda key: key >= cand), tq, topk)
    need = float(topk) - count(lambda key: key > v)
    before = _before_matrix()

    def sel_body(kb, run):
        key = keys_ref[kb]
        eqf = (key == v).astype(F32)
        rank = _mm(eqf, before) + run
        sel = (key > v) | ((key == v) & (rank < need))
        mask_ref[kb] = jnp.where(sel & valid_fn(kb), 0.0, NEG)
        return run + _rowsum(eqf)

    lax.fori_loop(0, nblk, sel_body, jnp.zeros((tq, 1), F32))


def _topk_select_wide(keys_ref, valid, topk, tq):
    def count(pred):
        return _rowsum(pred(keys_ref[...]).astype(F32))

    v = _kth_largest(lambda cand: count(lambda key: key >= cand), tq, topk)
    need = float(topk) - count(lambda key: key > v)
    before = _before_matrix()
    run = jnp.zeros((tq, 1), F32)
    masks = []
    for kb in range(keys_ref.shape[1] // BLK):
        sl = slice(kb * BLK, (kb + 1) * BLK)
        key = keys_ref[:, sl]
        eqf = (key == v).astype(F32)
        rank = _mm(eqf, before) + run
        sel = (key > v) | ((key == v) & (rank < need))
        if valid[kb] is not None:
            sel = sel & valid[kb]
        masks.append(jnp.where(sel, 0.0, NEG))
        run = run + _rowsum(eqf)
    return jnp.concatenate(masks, axis=1)


def _stacked_index_scores(q_stack, kk, wi, tq):
    s = jnp.maximum(_mm(q_stack, kk), 0.0)
    score = None
    for n in range(N_IDX_HEADS):
        term = s[n * tq:(n + 1) * tq] * wi[:, n:n + 1]
        score = term if score is None else score + term
    return score


def _mod_kernel(c_ref, w_ref, b_ref, o_ref):
    c = c_ref[...]
    o_ref[...] = _mm3(c * _sigmoid(c), w_ref[...]) + b_ref[...]


def _modulation(c_all, w_mod, b_mod):
    depth, d, _ = w_mod.shape
    n = c_all.shape[0]
    return pl.pallas_call(
        _mod_kernel,
        grid=(depth, 6),
        in_specs=[pl.BlockSpec((n, d), lambda l, c: (0, 0)),
                  pl.BlockSpec((None, d, d), lambda l, c: (l, 0, c)),
                  pl.BlockSpec((None, None, 1, d), lambda l, c: (l, c, 0, 0))],
        out_specs=pl.BlockSpec((None, None, n, d), lambda l, c: (l, c, 0, 0)),
        out_shape=jax.ShapeDtypeStruct((depth, 6, n, d), F32),
        compiler_params=pltpu.CompilerParams(dimension_semantics=("arbitrary", "arbitrary"),
                                             vmem_limit_bytes=VMEM_LIMIT),
        name="modulation",
    )(c_all, w_mod, b_mod.reshape(depth, 6, 1, d))


def _bias_kernel(rb_ref, o_ref):
    h = pl.program_id(0)
    t = lax.broadcasted_iota(I32, (BLK, BLK), 0)
    s = lax.broadcasted_iota(I32, (BLK, BLK), 1)
    for ty, c in ((0, 0), (1, BLK)):
        rel = c + t - s
        n = jnp.maximum(rel, 0)
        large = jnp.full((BLK, BLK), NUM_BUCKETS // 2, I32)
        for thr in _THRS:
            large = large + (n >= thr).astype(I32)
        bucket = jnp.where(n < NUM_BUCKETS // 2, n, large)
        val = jnp.zeros((BLK, BLK), F32)
        for b in range(NUM_BUCKETS):
            val = jnp.where(bucket == b, rb_ref[b, h], val)
        if ty == 0:
            val = jnp.where(rel >= 0, val, NEG)
        o_ref[ty] = val
    o_ref[2] = jnp.full((BLK, BLK), rb_ref[NUM_BUCKETS - 1, h], F32)
    o_ref[3] = jnp.full((BLK, BLK), NEG, F32)


def _bias_tiles(rel_bias):
    nh = rel_bias.shape[1]
    return pl.pallas_call(
        _bias_kernel,
        grid=(nh,),
        in_specs=[pl.BlockSpec(memory_space=pltpu.SMEM)],
        out_specs=pl.BlockSpec((None, 4, BLK, BLK), lambda h: (h, 0, 0, 0)),
        out_shape=jax.ShapeDtypeStruct((nh, 4, BLK, BLK), F32),
        name="bias_tiles",
    )(rel_bias)


_Q_OUTS = (("qa", BF16), ("qi", BF16), ("wi", F32), ("qd", BF16), ("qc", BF16))
_KB_OUTS = ("ka", "va", "ki2", "kd", "vd", "kc", "vc")


def _in_kernel(x_ref, g_ref, sh_ref, sc_ref, wq_ref, wk_ref, *outs, tiled):
    x = x_ref[...]
    h = (_rmsnorm(x, g_ref[...]) * (1.0 + sc_ref[...]) + sh_ref[...]).astype(BF16)
    yq = _mm(h, wq_ref[...])
    yk = _mm_nt(wk_ref[...], h)
    outs = list(outs)
    for name, dt in _Q_OUTS:
        off, w = _QCOLS[name]
        outs.pop(0)[...] = yq[:, off:off + w].astype(dt)
    for name in _KV:
        off, w = _krow(name)
        outs.pop(0)[...] = yk[off:off + w, :]
    if tiled:
        for name in _KB_OUTS:
            off, w = _KROWS[name]
            o = outs.pop(0)
            for c in range(o.shape[0]):
                o[c] = yk[off:off + w, c * BLK:(c + 1) * BLK].astype(BF16)


def _mod_spec(tm, d, rows_per_mod, l, c):
    if rows_per_mod is None:
        return pl.BlockSpec((None, None, None, tm, d), lambda i: (l, c, 0, i, 0))
    return pl.BlockSpec((None, None, None, 1, d), lambda i: (l, c, i // rows_per_mod, 0, 0))


def _in_proj(x, g, mod, wq, wk, l, nbatch, tm, tiles_per_mod, tiled):
    m, d = x.shape
    t = m // nbatch
    tpb = t // tm
    out_shapes, out_specs = [], []
    for name, dt in _Q_OUTS:
        w = _QCOLS[name][1]
        out_shapes.append(jax.ShapeDtypeStruct((m, w), dt))
        out_specs.append(pl.BlockSpec((tm, w), lambda i: (i, 0)))
    for name in _KV:
        w = _krow(name)[1]
        out_shapes.append(jax.ShapeDtypeStruct((nbatch, w, t), F32))
        out_specs.append(pl.BlockSpec((None, w, tm), lambda i: (i // tpb, 0, i % tpb)))
    if tiled:
        for name in _KB_OUTS:
            w = _KROWS[name][1]
            out_shapes.append(jax.ShapeDtypeStruct((nbatch, t // BLK, w, BLK), BF16))
            out_specs.append(pl.BlockSpec((None, tm // BLK, w, BLK), lambda i: (i // tpb, i % tpb, 0, 0)))
    outs = pl.pallas_call(
        functools.partial(_in_kernel, tiled=tiled),
        grid=(m // tm,),
        in_specs=[pl.BlockSpec((tm, d), lambda i: (i, 0)),
                  pl.BlockSpec((None, 1, d), lambda i: (l, 0, 0)),
                  _mod_spec(tm, d, tiles_per_mod, l, 0),
                  _mod_spec(tm, d, tiles_per_mod, l, 1),
                  pl.BlockSpec((None, d, N_Q), lambda i: (l, 0, 0)),
                  pl.BlockSpec((None, N_K, d), lambda i: (l, 0, 0))],
        out_specs=out_specs,
        out_shape=out_shapes,
        compiler_params=pltpu.CompilerParams(dimension_semantics=("arbitrary",),
                                             vmem_limit_bytes=VMEM_LIMIT),
        name="in_proj",
    )(x, g, mod, mod, wq, wk)
    names = [n for n, _ in _Q_OUTS] + list(_KV) + ([n + "_t" for n in _KB_OUTS] if tiled else [])
    return dict(zip(names, outs))


def _out_kernel(mix_ref, x_ref, gt_ref, w_ref, o_ref):
    o_ref[...] = x_ref[...] + gt_ref[...] * _mm(mix_ref[...], w_ref[...])


def _out_proj(mix, x, mod, w_out, l, tm, tiles_per_mod):
    m, d = x.shape
    return pl.pallas_call(
        _out_kernel,
        grid=(m // tm,),
        in_specs=[pl.BlockSpec((tm, d), lambda i: (i, 0)),
                  pl.BlockSpec((tm, d), lambda i: (i, 0)),
                  _mod_spec(tm, d, tiles_per_mod, l, 2),
                  pl.BlockSpec((None, d, d), lambda i: (l, 0, 0))],
        out_specs=pl.BlockSpec((tm, d), lambda i: (i, 0)),
        out_shape=jax.ShapeDtypeStruct((m, d), F32),
        compiler_params=pltpu.CompilerParams(dimension_semantics=("arbitrary",),
                                             vmem_limit_bytes=VMEM_LIMIT),
        name="out_proj",
    )(mix, x, mod, w_out)


def _ffn_kernel(x_ref, g_ref, sh_ref, sc_ref, gt_ref, wg_ref, wu_ref, wo_ref, gf_ref, o_ref,
                h_ref, acc_ref, *, final):
    j = pl.program_id(1)

    @pl.when(j == 0)
    def _():
        h = _rmsnorm(x_ref[...], g_ref[...]) * (1.0 + sc_ref[...]) + sh_ref[...]
        h_ref[...] = h.astype(BF16)
        acc_ref[...] = jnp.zeros_like(acc_ref)

    h = h_ref[...]
    gate = _mm(h, wg_ref[...])
    up = _mm(h, wu_ref[...])
    acc_ref[...] += _mm(gate * _sigmoid(gate) * up, wo_ref[...])

    @pl.when(j == pl.num_programs(1) - 1)
    def _():
        y = x_ref[...] + gt_ref[...] * acc_ref[...]
        if final:
            y = _rmsnorm(y, gf_ref[...])
        o_ref[...] = y


def _ffn(x, g, mod, w_in, w_out, g_final, l, tm, tiles_per_mod, final):
    m, d = x.shape
    dff = w_out.shape[1]
    nf = 2 if dff % (2 * LANES) == 0 else 1
    tf = dff // nf

    def ms(c):
        spec = _mod_spec(tm, d, tiles_per_mod, l, c)
        return pl.BlockSpec(spec.block_shape, lambda i, j, f=spec.index_map: f(i))

    return pl.pallas_call(
        functools.partial(_ffn_kernel, final=final),
        grid=(m // tm, nf),
        in_specs=[pl.BlockSpec((tm, d), lambda i, j: (i, 0)),
                  pl.BlockSpec((None, 1, d), lambda i, j: (l, 0, 0)),
                  ms(3), ms(4), ms(5),
                  pl.BlockSpec((None, d, tf), lambda i, j: (l, 0, j)),
                  pl.BlockSpec((None, d, tf), lambda i, j: (l, 0, nf + j)),
                  pl.BlockSpec((None, tf, d), lambda i, j: (l, j, 0)),
                  pl.BlockSpec((1, d), lambda i, j: (0, 0))],
        out_specs=pl.BlockSpec((tm, d), lambda i, j: (i, 0)),
        out_shape=jax.ShapeDtypeStruct((m, d), F32),
        scratch_shapes=[pltpu.VMEM((tm, d), BF16), pltpu.VMEM((tm, d), F32)],
        compiler_params=pltpu.CompilerParams(dimension_semantics=("arbitrary", "arbitrary"),
                                             vmem_limit_bytes=VMEM_LIMIT),
        name="ffn",
    )(x, g, mod, mod, mod, w_in, w_in, w_out, g_final)


def _lambda(dl_ref, lam_init):
    dl = dl_ref[...]
    a = jnp.sum(dl[0:1] * dl[1:2], axis=(0, 1), keepdims=True)
    b = jnp.sum(dl[2:3] * dl[3:4], axis=(0, 1), keepdims=True)
    return jnp.exp(a) - jnp.exp(b) + lam_init


def _blk(p):
    return slice(p * LANES, (p + 1) * LANES)


def _attn_p_kernel(qa_ref, ka_ref, va_ref, qi_ref, wi_ref, ki2_ref, qd_ref, kd_ref, vd_ref,
                   qc_ref, kc_ref, vc_ref, ta_ref, tb_ref, dl_ref, g2_ref, o_ref,
                   keys_ref, mask_ref, ma, la, acca, mb, lb, accb, cc, accc, *, topk, lam_init, kt):
    i = pl.program_id(1)
    tq = BLK
    nblk = i + 1
    t_loc = lax.broadcasted_iota(I32, (tq, BLK), 0)
    s_loc = lax.broadcasted_iota(I32, (tq, BLK), 1)

    def causal(kb):
        return kb * BLK + s_loc <= i * BLK + t_loc

    qi2 = [_stack_q(qi_ref[:, _blk(b)], 2) for b in range(N_IDX_HEADS // 2)]
    wi = wi_ref[...]

    def score_body(kb, carry):
        kk = ki2_ref[kb]
        score = None
        for b in range(N_IDX_HEADS // 2):
            s = jnp.maximum(_mm(qi2[b], kk), 0.0)
            term = s[:tq] * wi[:, 2 * b:2 * b + 1] + s[tq:] * wi[:, 2 * b + 1:2 * b + 2]
            score = term if score is None else score + term
        keys_ref[kb] = jnp.where(causal(kb), _sortable(score), INT_MIN)
        return carry

    lax.fori_loop(0, nblk, score_body, 0)
    _topk_select_blocks(keys_ref, mask_ref, nblk, topk, tq, causal)

    qa2 = [_stack_q(qa_ref[:, _blk(p)], 2) for p in range(NPA)]
    qd4 = [_stack_q(qd_ref[:, _blk(u)], 4) for u in range(NPB)]
    qc2 = [_stack_q(qc_ref[:, _blk(p)], 2) for p in range(NPC)]
    later_mat = _later_matrix()
    _softmax_init(ma, la, acca)
    _softmax_init(mb, lb, accb)
    cc[...] = jnp.zeros(cc.shape, F32)
    accc[...] = jnp.zeros(accc.shape, F32)
    t2 = lax.broadcasted_iota(I32, (2 * tq, BLK), 0) & (tq - 1)
    s2 = lax.broadcasted_iota(I32, (2 * tq, BLK), 1)

    def body(r, carry):
        first = (i // kt - r) * kt
        kbs = [first + c for c in range(kt)]
        tys = [jnp.where(kb > i, 3, jnp.minimum(i - kb, 2)) for kb in kbs]

        def cat(fn):
            parts = [fn(c) for c in range(kt)]
            return parts[0] if kt == 1 else jnp.concatenate(parts, axis=1)

        def keys(ref, rows):
            return cat(lambda c: ref[kbs[c], rows, :])

        def twice(x):
            return jnp.concatenate([x, x], axis=0)

        am2 = cat(lambda c: twice(mask_ref[jnp.minimum(kbs[c], i)]))
        for p in range(NPA):
            _softmax_update(ma, la, acca, p, qa2[p], keys(ka_ref, _blk(p)), keys(va_ref, _blk(p)),
                            cat(lambda c: ta_ref[p, tys[c]]) + am2)
        for u in range(NPB):
            _softmax_update(mb, lb, accb, u, qd4[u], keys(kd_ref, _blk(u)), keys(vd_ref, _blk(u)),
                            cat(lambda c: tb_ref[u, tys[c]]))
        strict = cat(lambda c: kbs[c] * BLK + s2 < i * BLK + t2)
        for p in range(NPC):
            _stick_update(cc, accc, p, qc2[p], keys(kc_ref, _blk(p)), keys(vc_ref, _blk(p)),
                          strict, later_mat)
        return carry

    lax.fori_loop(0, i // kt + 1, body, 0)

    for p in range(NPA):
        o_ref[:, _blk(p)] = _pair_out(acca[p], la[p], tq).astype(o_ref.dtype)
    lam = _lambda(dl_ref, lam_init)
    g2 = g2_ref[...]
    for u in range(NPB):
        o_ref[:, _blk(NPA + u)] = _diff_out(accb[u], lb[u], tq, lam, g2, 1.0 - lam_init).astype(o_ref.dtype)
    for p in range(NPC):
        acc = accc[p]
        o_ref[:, _blk(NPA + NPB + p)] = _sel2(acc[:tq], acc[tq:]).astype(o_ref.dtype)


def _attn_prompt(pr, ta, tb, dl, g2, l, b, t, topk, lam_init):
    nq = t // BLK

    def qspec(w):
        return pl.BlockSpec((None, BLK, w), lambda bb, i: (bb, i, 0))

    def kspec(w):
        return pl.BlockSpec((None, nq, w, BLK), lambda bb, i: (bb, 0, 0, 0))

    def full(a):
        return pl.BlockSpec(a.shape, lambda bb, i, n=a.ndim: (0,) * n)

    r3 = lambda a: a.reshape(b, t, a.shape[-1])
    args = [r3(pr["qa"]), pr["ka_t"], pr["va_t"], r3(pr["qi"]), r3(pr["wi"]), pr["ki2_t"],
            r3(pr["qd"]), pr["kd_t"], pr["vd_t"], r3(pr["qc"]), pr["kc_t"], pr["vc_t"]]
    specs = [qspec(384), kspec(384), kspec(384), qspec(256), qspec(128), kspec(128),
             qspec(256), kspec(256), kspec(256), qspec(384), kspec(384), kspec(384)]
    col = lambda n, r: pltpu.VMEM((n, r, LANES), F32)
    acc = lambda n, r: pltpu.VMEM((n, r, LANES), F32)
    return pl.pallas_call(
        functools.partial(_attn_p_kernel, topk=topk, lam_init=lam_init, kt=KT_PROMPT),
        grid=(b, nq),
        in_specs=specs + [full(ta), full(tb),
                          pl.BlockSpec((None, 4, HD_B), lambda bb, i: (l, 0, 0)),
                          pl.BlockSpec((None, 1, LANES), lambda bb, i: (l, 0, 0))],
        out_specs=pl.BlockSpec((None, BLK, 1024), lambda bb, i: (bb, i, 0)),
        out_shape=jax.ShapeDtypeStruct((b, t, 1024), BF16),
        scratch_shapes=[pltpu.VMEM((nq, BLK, BLK), I32), pltpu.VMEM((nq, BLK, BLK), F32),
                        col(NPA, 2 * BLK), col(NPA, 2 * BLK), acc(NPA, 2 * BLK),
                        col(NPB, 4 * BLK), col(NPB, 4 * BLK), acc(NPB, 4 * BLK),
                        col(NPC, 2 * BLK), acc(NPC, 2 * BLK)],
        compiler_params=pltpu.CompilerParams(dimension_semantics=("arbitrary", "arbitrary"),
                                             vmem_limit_bytes=VMEM_LIMIT),
        name="attn_prompt",
    )(*args, ta, tb, dl, g2)


def _sel_s_kernel(pt_ref, qi_ref, wi_ref, kn_ref, *rest, g, n_pages, topk):
    page_refs = rest[:g]
    o_ref = rest[g]
    keys_ref, wide_ref = rest[g + 1], rest[g + 2]
    j = pl.program_id(1)
    ns = n_pages // g
    q = qi_ref[...]
    wi = wi_ref[...]
    kk = jnp.concatenate([page_refs[r][...].astype(BF16) for r in range(g)], axis=1)
    keys_ref[j] = _sortable(_stacked_index_scores(q, kk, wi, SQ))

    @pl.when(j == ns - 1)
    def _():
        t_loc = lax.broadcasted_iota(I32, (SQ, BLK), 0)
        s_loc = lax.broadcasted_iota(I32, (SQ, BLK), 1)
        new_ok = s_loc <= t_loc
        for jj in range(ns):
            wide_ref[:, jj * g * BLK:(jj + 1) * g * BLK] = keys_ref[jj]
        score = _stacked_index_scores(q, kn_ref[...], wi, SQ)
        wide_ref[:, n_pages * BLK:] = jnp.where(new_ok, _sortable(score), INT_MIN)
        o_ref[...] = _topk_select_wide(wide_ref, [None] * n_pages + [new_ok], topk, SQ)


def _select_sample(page_table, qi_stack, wi, ki_new, pool_kidx, l, g, topk):
    nb, n_pages = page_table.shape
    ns = n_pages // g
    width = (n_pages + 1) * BLK

    def page_spec(r):
        return pl.BlockSpec((None, None, IDX_DIM, BLK),
                            lambda bb, j, pt, r=r: (l, pt[bb, j * g + r], 0, 0))

    grid_spec = pltpu.PrefetchScalarGridSpec(
        num_scalar_prefetch=1,
        grid=(nb, ns),
        in_specs=[pl.BlockSpec((None, N_IDX_HEADS * SQ, IDX_DIM), lambda bb, j, pt: (bb, 0, 0)),
                  pl.BlockSpec((None, SQ, LANES), lambda bb, j, pt: (bb, 0, 0)),
                  pl.BlockSpec((None, IDX_DIM, BLK), lambda bb, j, pt: (bb, 0, 0))]
                 + [page_spec(r) for r in range(g)],
        out_specs=pl.BlockSpec((None, SQ, width), lambda bb, j, pt: (bb, 0, 0)),
        scratch_shapes=[pltpu.VMEM((ns, SQ, g * BLK), I32), pltpu.VMEM((SQ, width), I32)],
    )
    return pl.pallas_call(
        functools.partial(_sel_s_kernel, g=g, n_pages=n_pages, topk=topk),
        grid_spec=grid_spec,
        out_shape=jax.ShapeDtypeStruct((nb, SQ, width), F32),
        compiler_params=pltpu.CompilerParams(dimension_semantics=("arbitrary", "arbitrary"),
                                             vmem_limit_bytes=VMEM_LIMIT),
        name="select_sample",
    )(page_table, qi_stack, wi, ki_new, *([pool_kidx] * g))


def _attn_s_kernel(pt_ref, qa_ref, qd_ref, qc_ref, kan_ref, van_ref, kdn_ref, vdn_ref, kcn_ref,
                   vcn_ref, mnew_ref, mask_ref, ta_ref, tb_ref, dl_ref, g2_ref, *rest,
                   g, n_pages, lam_init):
    pools = [rest[k * g:(k + 1) * g] for k in range(6)]
    o_ref = rest[6 * g]
    ma, la, acca, mb, lb, accb, cc, accc = rest[6 * g + 1:]
    j = pl.program_id(1)
    ns = pl.num_programs(1)
    tq = SQ
    later_mat = _later_matrix()
    qa2 = [_stack_q(qa_ref[:, _blk(p)], 2) for p in range(NPA)]
    qd4 = [_stack_q(qd_ref[:, _blk(u)], 4) for u in range(NPB)]
    qc2 = [_stack_q(qc_ref[:, _blk(p)], 2) for p in range(NPC)]

    def twice(x):
        return jnp.concatenate([x, x], axis=0)

    @pl.when(j == 0)
    def _():
        _softmax_init(ma, la, acca)
        _softmax_init(mb, lb, accb)
        cc[...] = jnp.zeros(cc.shape, F32)
        accc[...] = jnp.zeros(accc.shape, F32)
        am2 = twice(mnew_ref[...])
        for p in range(NPA):
            _softmax_update(ma, la, acca, p, qa2[p], kan_ref[_blk(p), :], van_ref[_blk(p), :],
                            ta_ref[p, 0] + am2)
        for u in range(NPB):
            _softmax_update(mb, lb, accb, u, qd4[u], kdn_ref[_blk(u), :], vdn_ref[_blk(u), :],
                            tb_ref[u, 0])
        t2 = lax.broadcasted_iota(I32, (2 * tq, BLK), 0) & (tq - 1)
        s2 = lax.broadcasted_iota(I32, (2 * tq, BLK), 1)
        for p in range(NPC):
            _stick_update(cc, accc, p, qc2[p], kcn_ref[_blk(p), :], vcn_ref[_blk(p), :],
                          s2 < t2, later_mat)

    def pages(kind, rows):
        return jnp.concatenate([pools[kind][r][rows, :].astype(BF16) for r in range(g)], axis=1)

    ty_last = jnp.where(j == 0, 1, 2)

    def bias(ref, idx):
        return jnp.concatenate([ref[idx, 2]] * (g - 1) + [ref[idx, ty_last]], axis=1)

    am2 = twice(mask_ref[...])
    for p in range(NPA):
        _softmax_update(ma, la, acca, p, qa2[p], pages(0, _blk(p)), pages(1, _blk(p)),
                        bias(ta_ref, p) + am2)
    for u in range(NPB):
        _softmax_update(mb, lb, accb, u, qd4[u], pages(2, _blk(u)), pages(3, _blk(u)), bias(tb_ref, u))
    for p in range(NPC):
        _stick_update(cc, accc, p, qc2[p], pages(4, _blk(p)), pages(5, _blk(p)), None, later_mat)

    @pl.when(j == ns - 1)
    def _():
        lam = _lambda(dl_ref, lam_init)
        g2 = g2_ref[...]
        for p in range(NPA):
            o_ref[:, _blk(p)] = _pair_out(acca[p], la[p], tq).astype(o_ref.dtype)
        for u in range(NPB):
            o_ref[:, _blk(NPA + u)] = _diff_out(accb[u], lb[u], tq, lam, g2,
                                               1.0 - lam_init).astype(o_ref.dtype)
        for p in range(NPC):
            acc = accc[p]
            o_ref[:, _blk(NPA + NPB + p)] = _sel2(acc[:tq], acc[tq:]).astype(o_ref.dtype)


def _attn_sample(page_table, q_arrays, new_arrays, addmask, ta, tb, dl, g2, pools, l, g, lam_init):
    nb, n_pages = page_table.shape
    ns = n_pages // g

    def seq_spec(a):
        return pl.BlockSpec((None,) + a.shape[1:], lambda bb, j, pt: (bb, 0, 0))

    def full(a):
        return pl.BlockSpec(a.shape, lambda bb, j, pt, n=a.ndim: (0,) * n)

    def page_spec(w, r):
        return pl.BlockSpec((None, None, w, BLK),
                            lambda bb, j, pt, r=r: (l, pt[bb, n_pages - (j + 1) * g + r], 0, 0))

    in_specs = [seq_spec(a) for a in q_arrays] + [seq_spec(a) for a in new_arrays]
    in_specs += [pl.BlockSpec((None, SQ, BLK), lambda bb, j, pt: (bb, 0, n_pages)),
                 pl.BlockSpec((None, SQ, g * BLK), lambda bb, j, pt: (bb, 0, ns - 1 - j)),
                 full(ta), full(tb),
                 pl.BlockSpec((None, 4, HD_B), lambda bb, j, pt: (l, 0, 0)),
                 pl.BlockSpec((None, 1, LANES), lambda bb, j, pt: (l, 0, 0))]
    args = list(q_arrays) + list(new_arrays) + [addmask, addmask, ta, tb, dl, g2]
    for pool in pools:
        w = pool.shape[-2]
        in_specs += [page_spec(w, r) for r in range(g)]
        args += [pool] * g
    col = lambda n, r: pltpu.VMEM((n, r, LANES), F32)
    acc = lambda n, r: pltpu.VMEM((n, r, LANES), F32)
    grid_spec = pltpu.PrefetchScalarGridSpec(
        num_scalar_prefetch=1,
        grid=(nb, ns),
        in_specs=in_specs,
        out_specs=pl.BlockSpec((None, SQ, 1024), lambda bb, j, pt: (bb, 0, 0)),
        scratch_shapes=[col(NPA, 2 * SQ), col(NPA, 2 * SQ), acc(NPA, 2 * SQ),
                        col(NPB, 4 * SQ), col(NPB, 4 * SQ), acc(NPB, 4 * SQ),
                        col(NPC, 2 * SQ), acc(NPC, 2 * SQ)],
    )
    return pl.pallas_call(
        functools.partial(_attn_s_kernel, g=g, n_pages=n_pages, lam_init=lam_init),
        grid_spec=grid_spec,
        out_shape=jax.ShapeDtypeStruct((nb, SQ, 1024), BF16),
        compiler_params=pltpu.CompilerParams(dimension_semantics=("arbitrary", "arbitrary"),
                                             vmem_limit_bytes=VMEM_LIMIT),
        name="attn_sample",
    )(page_table, *args)


def _proj_weights(w_in):
    splits = (384, 384, 384, 256, 4, 64, 256, 256, 256, 384, 384, 384)
    offs = np.concatenate([[0], np.cumsum(splits)])
    part = {n: w_in[:, :, offs[k]:offs[k + 1]] for k, n in enumerate(
        ("qa", "ka", "va", "qi", "wi", "ki", "qd", "kd", "vd", "qc", "kc", "vc"))}
    part["qa"] = part["qa"] * HEAD_DIM ** -0.5
    part["qc"] = part["qc"] * HEAD_DIM ** -0.5
    part["qi"] = part["qi"] * IDX_DIM ** -0.5
    part["qd"] = part["qd"] * HD_B ** -0.5
    part["wi"] = jnp.pad(part["wi"] * N_IDX_HEADS ** -0.5, ((0, 0), (0, 0), (0, LANES - N_IDX_HEADS)))
    part["ki2"] = jnp.concatenate([part["ki"], part["ki"]], axis=-1)
    wq = jnp.concatenate([part[n] for n in _QCOLS], axis=-1).astype(BF16)
    wk = jnp.swapaxes(jnp.concatenate([part[n] for n in _KROWS], axis=-1), 1, 2).astype(BF16)
    return wq, wk


def _row_tile(m):
    for tm in (256, 128, 64, 32, 16, 8):
        if m % tm == 0:
            return tm
    raise ValueError(m)


def _feature_major(cache):
    nd = cache.ndim
    c = jnp.transpose(cache, (0, 1) + tuple(range(3, nd)) + (2,))
    return c.reshape(c.shape[0], c.shape[1], -1, c.shape[-1])


def kernel(x_prompt, x_sample, c_prompt, c_sample, cache_a_k, cache_a_v, cache_a_kidx, cache_b_k,
           cache_b_v, cache_c_k, cache_c_v, page_table, rel_bias, w_mod, b_mod, g_attn, g_ffn, w_in,
           diff_lambda, subln_g, w_out, w_ffn_in, w_ffn_out, g_final):
    b, t, d = x_prompt.shape
    nb, ts, _ = x_sample.shape
    depth = w_in.shape[0]
    page = cache_a_k.shape[2]
    n_pages = page_table.shape[1]
    assert page == BLK and t % (KT_PROMPT * BLK) == 0 and ts <= 8 and d == 1024
    g_att = 8 if n_pages % 8 == 0 else (4 if n_pages % 4 == 0 else 1)
    g_sel = 16 if n_pages % 16 == 0 else g_att
    topk_p = min(TOPK_MAX, t // 4)
    topk_s = min(TOPK_MAX, (n_pages * page + ts) // 4)

    wq, wk = _proj_weights(w_in)
    w_out_b = w_out.astype(BF16)
    w_ffn_in_b = w_ffn_in.astype(BF16)
    w_ffn_out_b = w_ffn_out.astype(BF16)
    g_attn3 = g_attn.reshape(depth, 1, d)
    g_ffn3 = g_ffn.reshape(depth, 1, d)
    g_final2 = g_final.reshape(1, d)
    g2 = jnp.concatenate([subln_g, subln_g], axis=-1).reshape(depth, 1, LANES)
    pools = [_feature_major(c) for c in (cache_a_k, cache_a_v, cache_b_k, cache_b_v, cache_c_k, cache_c_v)]
    pool_kidx = _feature_major(cache_a_kidx)

    mod = _modulation(jnp.concatenate([c_prompt, c_sample], axis=0), w_mod, b_mod)
    mod_p = mod[:, :, :b].reshape(depth, 6, b, 1, d)
    mod_s = jnp.repeat(mod[:, :, b:], ts, axis=2).reshape(depth, 6, 1, nb * ts, d)

    tiles = _bias_tiles(rel_bias)

    def stacked(heads, rows):
        return jnp.concatenate([tiles[h][:, :rows] for h in heads], axis=1)

    pairs_a = [(2 * p, 2 * p + 1) for p in range(NPA)]
    quads_b = [(H_A + 2 * u,) * 2 + (H_A + 2 * u + 1,) * 2 for u in range(NPB)]
    ta_p = jnp.stack([stacked(hs, BLK) for hs in pairs_a])
    tb_p = jnp.stack([stacked(hs, BLK) for hs in quads_b])
    ta_s = jnp.stack([stacked(hs, SQ) for hs in pairs_a])
    tb_s = jnp.stack([stacked(hs, SQ) for hs in quads_b])

    xp = x_prompt.reshape(b * t, d)
    xs = x_sample.reshape(nb * ts, d)
    tm_p = _row_tile(t)
    tm_s = nb * ts
    new_p = {k: [] for k in _KV}
    new_s = {k: [] for k in _KV}

    def pad_q(a):
        return jnp.pad(a.reshape(nb, ts, -1), ((0, 0), (0, SQ - ts), (0, 0)))

    def new_keys(a, w=None):
        a = a[0] if w is None else a[0, :w]
        a = jnp.transpose(a.reshape(a.shape[0], nb, ts), (1, 0, 2))
        return jnp.pad(a, ((0, 0), (0, 0), (0, BLK - ts))).astype(BF16)

    for l in range(depth):
        lam_init = 0.8 - 0.6 * math.exp(-0.3 * l)
        final = l == depth - 1
        pr = _in_proj(xp, g_attn3, mod_p, wq, wk, l, b, tm_p, t // tm_p, True)
        mix = _attn_prompt(pr, ta_p, tb_p, diff_lambda, g2, l, b, t, topk_p, lam_init)
        xp = _out_proj(mix.reshape(b * t, d), xp, mod_p, w_out_b, l, tm_p, t // tm_p)
        xp = _ffn(xp, g_ffn3, mod_p, w_ffn_in_b, w_ffn_out_b, g_final2, l, tm_p, t // tm_p, final)
        sr = _in_proj(xs, g_attn3, mod_s, wq, wk, l, 1, tm_s, None, False)
        qi_stack = pad_q(sr["qi"]).reshape(nb, SQ, N_IDX_HEADS, IDX_DIM).transpose(0, 2, 1, 3)
        qi_stack = qi_stack.reshape(nb, N_IDX_HEADS * SQ, IDX_DIM)
        addmask = _select_sample(page_table, qi_stack, pad_q(sr["wi"]), new_keys(sr["ki"]),
                                 pool_kidx, l, g_sel, topk_s)
        mix_s = _attn_sample(page_table,
                             [pad_q(sr["qa"]), pad_q(sr["qd"]), pad_q(sr["qc"])],
                             [new_keys(sr[n]) for n in ("ka", "va", "kd", "vd", "kc", "vc")],
                             addmask, ta_s, tb_s, diff_lambda, g2, pools, l, g_att, lam_init)
        mix_s = mix_s[:, :ts].reshape(nb * ts, d)
        xs = _out_proj(mix_s, xs, mod_s, w_out_b, l, tm_s, None)
        xs = _ffn(xs, g_ffn3, mod_s, w_ffn_in_b, w_ffn_out_b, g_final2, l, tm_s, None, final)
        for k in _KV:
            new_p[k].append(pr[k])
            new_s[k].append(sr[k])

    tails = {"ka": (H_A, HEAD_DIM), "va": (H_A, HEAD_DIM), "ki": (IDX_DIM,), "kd": (H_B, 2, HD_B),
             "vd": (H_B, 2 * HD_B), "kc": (H_C, HEAD_DIM), "vc": (H_C, HEAD_DIM)}

    def rows_out(arrs, lead, tail):
        a = jnp.stack(arrs)
        nbt, tt = a.shape[1], a.shape[3]
        a = a.reshape((depth, nbt) + tail + (tt,))
        nd = a.ndim
        a = jnp.transpose(a, (0, 1, nd - 1) + tuple(range(2, nd - 1)))
        return a.reshape((depth,) + lead + tail)

    outs_p = [rows_out(new_p[k], (b, t), tails[k]) for k in _KV]
    outs_s = [rows_out(new_s[k], (nb, ts), tails[k]) for k in _KV]
    return (xp.reshape(b, t, d), xs.reshape(nb, ts, d), *outs_p, *outs_s)
```

```python
import functools
import math

import numpy as np
import jax
import jax.numpy as jnp
from jax import lax
from jax.experimental import pallas as pl
from jax.experimental.pallas import tpu as pltpu

F32, BF16, I32 = jnp.float32, jnp.bfloat16, jnp.int32

HEAD_DIM = 64
H_A, H_B, H_C = 6, 4, 6
HD_B = HEAD_DIM // 2
N_IDX_HEADS = 4
IDX_DIM = 64
TOPK_MAX = 256
NUM_BUCKETS = 32
MAX_DISTANCE = 128
EPS = 1e-6
NEG = -1e30
M_INIT = -3e38
INT_MIN = -2 ** 31
LANES = 128
BLK = 128
SQ = 16
KT_PROMPT = 4
VMEM_LIMIT = 56 * 1024 * 1024
NPA, NPB, NPC = H_A // 2, H_B // 2, H_C // 2


def _layout(groups):
    out, off = {}, 0
    for n, w in groups:
        out[n] = (off, w)
        off += w
    return out, off


_QCOLS, N_Q = _layout((("qa", 384), ("qi", 256), ("wi", 128), ("qd", 256), ("qc", 384)))
_KROWS, N_K = _layout((("ka", 384), ("va", 384), ("ki2", 128), ("kd", 256), ("vd", 256),
                       ("kc", 384), ("vc", 384)))
_KV = ("ka", "va", "ki", "kd", "vd", "kc", "vc")


def _krow(name):
    return (_KROWS["ki2"][0], IDX_DIM) if name == "ki" else _KROWS[name]


def _bucket_thresholds():
    max_exact = NUM_BUCKETS // 2
    n = np.arange(max_exact, 4 * MAX_DISTANCE, dtype=np.float32)
    large = max_exact + (np.log(n / max_exact) / math.log(MAX_DISTANCE / max_exact)
                         * (NUM_BUCKETS - max_exact)).astype(np.int32)
    large = np.minimum(large, NUM_BUCKETS - 1)
    return [int(n[np.argmax(large >= b)]) for b in range(max_exact + 1, NUM_BUCKETS)]


_THRS = _bucket_thresholds()


def _mm(a, b):
    return lax.dot_general(a.astype(BF16), b.astype(BF16), (((1,), (0,)), ((), ())),
                           preferred_element_type=F32)


def _mm_nt(a, b):
    return lax.dot_general(a.astype(BF16), b.astype(BF16), (((1,), (1,)), ((), ())),
                           preferred_element_type=F32)


def _split(a):
    hi = a.astype(BF16)
    lo = (a - hi.astype(F32)).astype(BF16)
    return hi, lo


def _mm3(a, b):
    ah, al = _split(a)
    bh, bl = _split(b)
    return _mm(ah, bh) + (_mm(ah, bl) + _mm(al, bh))


def _sigmoid(x):
    return 1.0 / (1.0 + jnp.exp(-x))


def _rmsnorm(x, g):
    return x * lax.rsqrt(jnp.mean(x * x, axis=-1, keepdims=True) + EPS) * g


def _lane():
    return lax.broadcasted_iota(I32, (1, LANES), 1)


def _stack_q(q, parts):
    grp = _lane() >> int(math.log2(LANES // parts))
    zero = jnp.zeros_like(q)
    return jnp.concatenate([jnp.where(grp == r, q, zero) for r in range(parts)], axis=0)


def _sel2(lo_val, hi_val):
    return jnp.where(_lane() < HEAD_DIM, lo_val, hi_val)


def _rowsum(x):
    return jnp.sum(x, axis=-1, keepdims=True)


def _tile_lanes(x, n):
    return x if n == 1 else jnp.concatenate([x] * n, axis=1)


def _softmax_update(m_ref, l_ref, acc_ref, idx, q_st, kk, vv, add, mxu_rowsum):
    s = _mm(q_st, kk) + add
    m_old = m_ref[idx]
    m_new = jnp.maximum(m_old, jnp.max(s, axis=-1, keepdims=True))
    p = jnp.exp(s - _tile_lanes(m_new, s.shape[1] // LANES)).astype(BF16)
    alpha = jnp.exp(m_old - m_new)
    if mxu_rowsum:
        both = _mm_nt(p, jnp.concatenate([vv.astype(BF16), jnp.ones(vv.shape, BF16)], axis=0))
        pv, psum = both[:, :LANES], both[:, LANES:]
    else:
        pv, psum = _mm_nt(p, vv), _rowsum(p.astype(F32))
    l_ref[idx] = alpha * l_ref[idx] + psum
    acc_ref[idx] = alpha * acc_ref[idx] + pv
    m_ref[idx] = m_new


def _softmax_init(m_ref, l_ref, acc_ref):
    m_ref[...] = jnp.full(m_ref.shape, M_INIT, F32)
    l_ref[...] = jnp.zeros(l_ref.shape, F32)
    acc_ref[...] = jnp.zeros(acc_ref.shape, F32)


def _pair_out(acc, l, tq):
    o = acc / l
    return _sel2(o[:tq], o[tq:])


def _diff_out(acc, l, tq, lam, g2, out_scale):
    o = acc / l
    o = _sel2(o[:tq], o[2 * tq:3 * tq]) - lam * _sel2(o[tq:2 * tq], o[3 * tq:])
    o2 = o * o
    lo = _lane() < HEAD_DIM
    ms_lo = _rowsum(jnp.where(lo, o2, 0.0)) * (1.0 / HEAD_DIM)
    ms_hi = _rowsum(jnp.where(lo, 0.0, o2)) * (1.0 / HEAD_DIM)
    r = _sel2(lax.rsqrt(ms_lo + EPS), lax.rsqrt(ms_hi + EPS))
    return o * r * g2 * out_scale


def _later_matrix():
    j = lax.broadcasted_iota(I32, (BLK, 2 * BLK), 0)
    s = lax.broadcasted_iota(I32, (BLK, 2 * BLK), 1)
    return ((j > s) | (s >= BLK)).astype(BF16)


def _stick_update(c_ref, acc_ref, q_sts, kks, vvs, strict, later_mat):
    n = kks[0].shape[1] // BLK
    rows = q_sts[0].shape[0]
    lbs, his, los = [], [], []
    for q_st, kk in zip(q_sts, kks):
        z = _mm(q_st, kk)
        lb = jnp.minimum(z, 0.0) - jnp.log(1.0 + jnp.exp(-jnp.abs(z)))
        lk = lb - z
        if strict is not None:
            lk = jnp.where(strict, lk, 0.0)
        hi, lo = _split(lk)
        lbs.append(lb)
        his += [hi[:, r * BLK:(r + 1) * BLK] for r in range(n)]
        los += [lo[:, r * BLK:(r + 1) * BLK] for r in range(n)]
    both = _mm(jnp.concatenate(his, axis=0), later_mat) + _mm(jnp.concatenate(los, axis=0), later_mat)
    for idx, (lb, vv) in enumerate(zip(lbs, vvs)):
        c = c_ref[idx]
        laters = [None] * n
        for r in reversed(range(n)):
            part = both[(idx * n + r) * rows:(idx * n + r + 1) * rows]
            laters[r] = part[:, :BLK] + c
            c = c + part[:, BLK:]
        later = laters[0] if n == 1 else jnp.concatenate(laters, axis=1)
        a = jnp.exp(lb + later)
        if strict is not None:
            a = jnp.where(strict, a, 0.0)
        acc_ref[idx] = acc_ref[idx] + _mm_nt(a, vv)
        c_ref[idx] = c


def _sortable(score):
    i = lax.bitcast_convert_type(score + 0.0, I32)
    return i ^ ((i >> 31) & 0x7FFFFFFF)


def _before_matrix():
    jj = lax.broadcasted_iota(I32, (BLK, BLK), 0)
    ss = lax.broadcasted_iota(I32, (BLK, BLK), 1)
    return (jj < ss).astype(BF16)


def _kth_largest(count_ge, tq, topk):
    kf = float(topk)
    zero = jnp.zeros((tq, 1), I32)
    v = jnp.where(count_ge(zero) >= kf, zero, jnp.full((tq, 1), INT_MIN, I32))

    def bit_body(bi, v):
        cand = v | lax.shift_left(jnp.int32(1), 30 - bi)
        return jnp.where(count_ge(cand) >= kf, cand, v)

    return lax.fori_loop(0, 31, bit_body, v)


def _topk_select_blocks(keys_ref, mask_ref, ngrp, grp, topk, tq, valid_fn):
    def count(pred):
        def body(gi, acc):
            for c in range(grp):
                acc = acc + pred(keys_ref[gi * grp + c]).astype(F32)
            return acc
        return _rowsum(lax.fori_loop(0, ngrp, body, jnp.zeros((tq, LANES), F32)))

    v = _kth_largest(lambda cand: count(lambda key: key >= cand), tq, topk)
    need = float(topk) - count(lambda key: key > v)
    before = _before_matrix()

    def sel_body(gi, run):
        keys = [keys_ref[gi * grp + c] for c in range(grp)]
        eqs = [(key == v).astype(F32) for key in keys]
        ranks = _mm(jnp.concatenate(eqs, axis=0), before)
        for c, (key, eqf) in enumerate(zip(keys, eqs)):
            kb = gi * grp + c
            rank = ranks[c * tq:(c + 1) * tq] + run
            sel = (key > v) | ((key == v) & (rank < need))
            mask_ref[kb] = jnp.where(sel & valid_fn(kb), 0.0, NEG)
            run = run + _rowsum(eqf)
        return run

    lax.fori_loop(0, ngrp, sel_body, jnp.zeros((tq, 1), F32))


def _topk_select_wide(keys_ref, valid, topk, tq):
    def count(pred):
        return _rowsum(pred(keys_ref[...]).astype(F32))

    v = _kth_largest(lambda cand: count(lambda key: key >= cand), tq, topk)
    need = float(topk) - count(lambda key: key > v)
    before = _before_matrix()
    run = jnp.zeros((tq, 1), F32)
    masks = []
    for kb in range(keys_ref.shape[1] // BLK):
        sl = slice(kb * BLK, (kb + 1) * BLK)
        key = keys_ref[:, sl]
        eqf = (key == v).astype(F32)
        rank = _mm(eqf, before) + run
        sel = (key > v) | ((key == v) & (rank < need))
        if valid[kb] is not None:
            sel = sel & valid[kb]
        masks.append(jnp.where(sel, 0.0, NEG))
        run = run + _rowsum(eqf)
    return jnp.concatenate(masks, axis=1)


def _stacked_index_scores(q_stack, kk, wi, tq):
    s = jnp.maximum(_mm(q_stack, kk), 0.0)
    score = None
    for n in range(N_IDX_HEADS):
        term = s[n * tq:(n + 1) * tq] * wi[:, n:n + 1]
        score = term if score is None else score + term
    return score


def _mod_kernel(c_ref, w_ref, b_ref, o_ref):
    c = c_ref[...]
    o_ref[...] = _mm3(c * _sigmoid(c), w_ref[...]) + b_ref[...]


def _modulation(c_all, w_mod, b_mod):
    depth, d, _ = w_mod.shape
    n = c_all.shape[0]
    return pl.pallas_call(
        _mod_kernel,
        grid=(depth, 6),
        in_specs=[pl.BlockSpec((n, d), lambda l, c: (0, 0)),
                  pl.BlockSpec((None, d, d), lambda l, c: (l, 0, c)),
                  pl.BlockSpec((None, None, 1, d), lambda l, c: (l, c, 0, 0))],
        out_specs=pl.BlockSpec((None, None, n, d), lambda l, c: (l, c, 0, 0)),
        out_shape=jax.ShapeDtypeStruct((depth, 6, n, d), F32),
        compiler_params=pltpu.CompilerParams(dimension_semantics=("arbitrary", "arbitrary"),
                                             vmem_limit_bytes=VMEM_LIMIT),
        name="modulation",
    )(c_all, w_mod, b_mod.reshape(depth, 6, 1, d))


def _bias_kernel(rb_ref, o_ref):
    h = pl.program_id(0)
    t = lax.broadcasted_iota(I32, (BLK, BLK), 0)
    s = lax.broadcasted_iota(I32, (BLK, BLK), 1)
    for ty, c in ((0, 0), (1, BLK)):
        rel = c + t - s
        n = jnp.maximum(rel, 0)
        large = jnp.full((BLK, BLK), NUM_BUCKETS // 2, I32)
        for thr in _THRS:
            large = large + (n >= thr).astype(I32)
        bucket = jnp.where(n < NUM_BUCKETS // 2, n, large)
        val = jnp.zeros((BLK, BLK), F32)
        for b in range(NUM_BUCKETS):
            val = jnp.where(bucket == b, rb_ref[b, h], val)
        if ty == 0:
            val = jnp.where(rel >= 0, val, NEG)
        o_ref[ty] = val
    o_ref[2] = jnp.full((BLK, BLK), rb_ref[NUM_BUCKETS - 1, h], F32)
    o_ref[3] = jnp.full((BLK, BLK), NEG, F32)


def _bias_tiles(rel_bias):
    nh = rel_bias.shape[1]
    return pl.pallas_call(
        _bias_kernel,
        grid=(nh,),
        in_specs=[pl.BlockSpec(memory_space=pltpu.SMEM)],
        out_specs=pl.BlockSpec((None, 4, BLK, BLK), lambda h: (h, 0, 0, 0)),
        out_shape=jax.ShapeDtypeStruct((nh, 4, BLK, BLK), F32),
        name="bias_tiles",
    )(rel_bias)


_Q_OUTS = (("qa", BF16), ("qi", BF16), ("wi", F32), ("qd", BF16), ("qc", BF16))
_KB_OUTS = ("ka", "va", "ki2", "kd", "vd", "kc", "vc")


def _in_kernel(x_ref, g_ref, sh_ref, sc_ref, wq_ref, wk_ref, *outs, tiled):
    x = x_ref[...]
    h = (_rmsnorm(x, g_ref[...]) * (1.0 + sc_ref[...]) + sh_ref[...]).astype(BF16)
    yq = _mm(h, wq_ref[...])
    yk = _mm_nt(wk_ref[...], h)
    outs = list(outs)
    for name, dt in _Q_OUTS:
        off, w = _QCOLS[name]
        outs.pop(0)[...] = yq[:, off:off + w].astype(dt)
    for name in _KV:
        off, w = _krow(name)
        outs.pop(0)[...] = yk[off:off + w, :]
    if tiled:
        for name in _KB_OUTS:
            off, w = _KROWS[name]
            o = outs.pop(0)
            for c in range(o.shape[0]):
                o[c] = yk[off:off + w, c * BLK:(c + 1) * BLK].astype(BF16)


def _mod_spec(tm, d, rows_per_mod, l, c):
    if rows_per_mod is None:
        return pl.BlockSpec((None, None, None, tm, d), lambda i: (l, c, 0, i, 0))
    return pl.BlockSpec((None, None, None, 1, d), lambda i: (l, c, i // rows_per_mod, 0, 0))


def _in_proj(x, g, mod, wq, wk, l, nbatch, tm, tiles_per_mod, tiled):
    m, d = x.shape
    t = m // nbatch
    tpb = t // tm
    out_shapes, out_specs = [], []
    for name, dt in _Q_OUTS:
        w = _QCOLS[name][1]
        out_shapes.append(jax.ShapeDtypeStruct((m, w), dt))
        out_specs.append(pl.BlockSpec((tm, w), lambda i: (i, 0)))
    for name in _KV:
        w = _krow(name)[1]
        out_shapes.append(jax.ShapeDtypeStruct((nbatch, w, t), F32))
        out_specs.append(pl.BlockSpec((None, w, tm), lambda i: (i // tpb, 0, i % tpb)))
    if tiled:
        for name in _KB_OUTS:
            w = _KROWS[name][1]
            out_shapes.append(jax.ShapeDtypeStruct((nbatch, t // BLK, w, BLK), BF16))
            out_specs.append(pl.BlockSpec((None, tm // BLK, w, BLK), lambda i: (i // tpb, i % tpb, 0, 0)))
    outs = pl.pallas_call(
        functools.partial(_in_kernel, tiled=tiled),
        grid=(m // tm,),
        in_specs=[pl.BlockSpec((tm, d), lambda i: (i, 0)),
                  pl.BlockSpec((None, 1, d), lambda i: (l, 0, 0)),
                  _mod_spec(tm, d, tiles_per_mod, l, 0),
                  _mod_spec(tm, d, tiles_per_mod, l, 1),
                  pl.BlockSpec((None, d, N_Q), lambda i: (l, 0, 0)),
                  pl.BlockSpec((None, N_K, d), lambda i: (l, 0, 0))],
        out_specs=out_specs,
        out_shape=out_shapes,
        compiler_params=pltpu.CompilerParams(dimension_semantics=("arbitrary",),
                                             vmem_limit_bytes=VMEM_LIMIT),
        name="in_proj",
    )(x, g, mod, mod, wq, wk)
    names = [n for n, _ in _Q_OUTS] + list(_KV) + ([n + "_t" for n in _KB_OUTS] if tiled else [])
    return dict(zip(names, outs))


def _out_kernel(mix_ref, x_ref, gt_ref, w_ref, o_ref):
    o_ref[...] = x_ref[...] + gt_ref[...] * _mm(mix_ref[...], w_ref[...])


def _out_proj(mix, x, mod, w_out, l, tm, tiles_per_mod):
    m, d = x.shape
    return pl.pallas_call(
        _out_kernel,
        grid=(m // tm,),
        in_specs=[pl.BlockSpec((tm, d), lambda i: (i, 0)),
                  pl.BlockSpec((tm, d), lambda i: (i, 0)),
                  _mod_spec(tm, d, tiles_per_mod, l, 2),
                  pl.BlockSpec((None, d, d), lambda i: (l, 0, 0))],
        out_specs=pl.BlockSpec((tm, d), lambda i: (i, 0)),
        out_shape=jax.ShapeDtypeStruct((m, d), F32),
        compiler_params=pltpu.CompilerParams(dimension_semantics=("arbitrary",),
                                             vmem_limit_bytes=VMEM_LIMIT),
        name="out_proj",
    )(mix, x, mod, w_out)


def _ffn_kernel(x_ref, g_ref, sh_ref, sc_ref, gt_ref, wg_ref, wu_ref, wo_ref, gf_ref, o_ref,
                h_ref, acc_ref, *, final):
    j = pl.program_id(1)

    @pl.when(j == 0)
    def _():
        h = _rmsnorm(x_ref[...], g_ref[...]) * (1.0 + sc_ref[...]) + sh_ref[...]
        h_ref[...] = h.astype(BF16)
        acc_ref[...] = jnp.zeros_like(acc_ref)

    h = h_ref[...]
    gate = _mm(h, wg_ref[...])
    up = _mm(h, wu_ref[...])
    acc_ref[...] += _mm(gate * _sigmoid(gate) * up, wo_ref[...])

    @pl.when(j == pl.num_programs(1) - 1)
    def _():
        y = x_ref[...] + gt_ref[...] * acc_ref[...]
        if final:
            y = _rmsnorm(y, gf_ref[...])
        o_ref[...] = y


def _ffn(x, g, mod, w_in, w_out, g_final, l, tm, tiles_per_mod, final):
    m, d = x.shape
    dff = w_out.shape[1]
    nf = 2 if dff % (2 * LANES) == 0 else 1
    tf = dff // nf

    def ms(c):
        spec = _mod_spec(tm, d, tiles_per_mod, l, c)
        return pl.BlockSpec(spec.block_shape, lambda i, j, f=spec.index_map: f(i))

    return pl.pallas_call(
        functools.partial(_ffn_kernel, final=final),
        grid=(m // tm, nf),
        in_specs=[pl.BlockSpec((tm, d), lambda i, j: (i, 0)),
                  pl.BlockSpec((None, 1, d), lambda i, j: (l, 0, 0)),
                  ms(3), ms(4), ms(5),
                  pl.BlockSpec((None, d, tf), lambda i, j: (l, 0, j)),
                  pl.BlockSpec((None, d, tf), lambda i, j: (l, 0, nf + j)),
                  pl.BlockSpec((None, tf, d), lambda i, j: (l, j, 0)),
                  pl.BlockSpec((1, d), lambda i, j: (0, 0))],
        out_specs=pl.BlockSpec((tm, d), lambda i, j: (i, 0)),
        out_shape=jax.ShapeDtypeStruct((m, d), F32),
        scratch_shapes=[pltpu.VMEM((tm, d), BF16), pltpu.VMEM((tm, d), F32)],
        compiler_params=pltpu.CompilerParams(dimension_semantics=("arbitrary", "arbitrary"),
                                             vmem_limit_bytes=VMEM_LIMIT),
        name="ffn",
    )(x, g, mod, mod, mod, w_in, w_in, w_out, g_final)


def _lambda(dl_ref, lam_init):
    dl = dl_ref[...]
    a = jnp.sum(dl[0:1] * dl[1:2], axis=(0, 1), keepdims=True)
    b = jnp.sum(dl[2:3] * dl[3:4], axis=(0, 1), keepdims=True)
    return jnp.exp(a) - jnp.exp(b) + lam_init


def _blk(p):
    return slice(p * LANES, (p + 1) * LANES)


def _attn_p_kernel(qa_ref, ka_ref, va_ref, qi_ref, wi_ref, ki2_ref, qd_ref, kd_ref, vd_ref,
                   qc_ref, kc_ref, vc_ref, ta_ref, tb_ref, dl_ref, g2_ref, o_ref,
                   keys_ref, mask_ref, ma, la, acca, mb, lb, accb, cc, accc, *, topk, lam_init, kt):
    i = pl.program_id(1)
    tq = BLK
    nblk = i + 1
    t_loc = lax.broadcasted_iota(I32, (tq, BLK), 0)
    s_loc = lax.broadcasted_iota(I32, (tq, BLK), 1)

    def causal(kb):
        return kb * BLK + s_loc <= i * BLK + t_loc

    qi2 = [_stack_q(qi_ref[:, _blk(b)], 2) for b in range(N_IDX_HEADS // 2)]
    wi = wi_ref[...]
    wib = [jnp.broadcast_to(wi[:, n:n + 1], (tq, kt * BLK)) for n in range(N_IDX_HEADS)]
    ngrp = i // kt + 1

    def score_body(gi, carry):
        kk = jnp.concatenate([ki2_ref[gi * kt + c] for c in range(kt)], axis=1)
        score = None
        for b in range(N_IDX_HEADS // 2):
            s = jnp.maximum(_mm(qi2[b], kk), 0.0)
            term = s[:tq] * wib[2 * b] + s[tq:] * wib[2 * b + 1]
            score = term if score is None else score + term
        key = _sortable(score)
        for c in range(kt):
            kb = gi * kt + c
            keys_ref[kb] = jnp.where(causal(kb), key[:, c * BLK:(c + 1) * BLK], INT_MIN)
        return carry

    lax.fori_loop(0, ngrp, score_body, 0)
    _topk_select_blocks(keys_ref, mask_ref, ngrp, kt, topk, tq, causal)

    qa2 = [_stack_q(qa_ref[:, _blk(p)], 2) for p in range(NPA)]
    qd4 = [_stack_q(qd_ref[:, _blk(u)], 4) for u in range(NPB)]
    qc2 = [_stack_q(qc_ref[:, _blk(p)], 2) for p in range(NPC)]
    later_mat = _later_matrix()
    _softmax_init(ma, la, acca)
    _softmax_init(mb, lb, accb)
    cc[...] = jnp.zeros(cc.shape, F32)
    accc[...] = jnp.zeros(accc.shape, F32)
    t2 = lax.broadcasted_iota(I32, (2 * tq, BLK), 0) & (tq - 1)
    s2 = lax.broadcasted_iota(I32, (2 * tq, BLK), 1)

    def body(r, carry, diagonal):
        first = (i // kt - r) * kt
        kbs = [first + c for c in range(kt)]
        tys = [jnp.where(kb > i, 3, jnp.minimum(i - kb, 2)) for kb in kbs]

        def cat(fn):
            parts = [fn(c) for c in range(kt)]
            return parts[0] if kt == 1 else jnp.concatenate(parts, axis=1)

        def keys(ref, rows):
            return cat(lambda c: ref[kbs[c], rows, :])

        def twice(x):
            return jnp.concatenate([x, x], axis=0)

        am2 = cat(lambda c: twice(mask_ref[jnp.minimum(kbs[c], i)]))
        for p in range(NPA):
            _softmax_update(ma, la, acca, p, qa2[p], keys(ka_ref, _blk(p)), keys(va_ref, _blk(p)),
                            cat(lambda c: ta_ref[p, tys[c]]) + am2, True)
        for u in range(NPB):
            _softmax_update(mb, lb, accb, u, qd4[u], keys(kd_ref, _blk(u)), keys(vd_ref, _blk(u)),
                            cat(lambda c: tb_ref[u, tys[c]]), True)
        strict = cat(lambda c: kbs[c] * BLK + s2 < i * BLK + t2) if diagonal else None
        _stick_update(cc, accc, qc2, [keys(kc_ref, _blk(p)) for p in range(NPC)],
                      [keys(vc_ref, _blk(p)) for p in range(NPC)], strict, later_mat)
        return carry

    body(0, 0, diagonal=True)
    lax.fori_loop(1, i // kt + 1, functools.partial(body, diagonal=False), 0)

    for p in range(NPA):
        o_ref[:, _blk(p)] = _pair_out(acca[p], la[p], tq).astype(o_ref.dtype)
    lam = _lambda(dl_ref, lam_init)
    g2 = g2_ref[...]
    for u in range(NPB):
        o_ref[:, _blk(NPA + u)] = _diff_out(accb[u], lb[u], tq, lam, g2, 1.0 - lam_init).astype(o_ref.dtype)
    for p in range(NPC):
        acc = accc[p]
        o_ref[:, _blk(NPA + NPB + p)] = _sel2(acc[:tq], acc[tq:]).astype(o_ref.dtype)


def _attn_prompt(pr, ta, tb, dl, g2, l, b, t, topk, lam_init):
    nq = t // BLK

    def qspec(w):
        return pl.BlockSpec((None, BLK, w), lambda bb, i: (bb, i, 0))

    def kspec(w):
        return pl.BlockSpec((None, nq, w, BLK), lambda bb, i: (bb, 0, 0, 0))

    def full(a):
        return pl.BlockSpec(a.shape, lambda bb, i, n=a.ndim: (0,) * n)

    r3 = lambda a: a.reshape(b, t, a.shape[-1])
    args = [r3(pr["qa"]), pr["ka_t"], pr["va_t"], r3(pr["qi"]), r3(pr["wi"]), pr["ki2_t"],
            r3(pr["qd"]), pr["kd_t"], pr["vd_t"], r3(pr["qc"]), pr["kc_t"], pr["vc_t"]]
    specs = [qspec(384), kspec(384), kspec(384), qspec(256), qspec(128), kspec(128),
             qspec(256), kspec(256), kspec(256), qspec(384), kspec(384), kspec(384)]
    col = lambda n, r: pltpu.VMEM((n, r, LANES), F32)
    acc = lambda n, r: pltpu.VMEM((n, r, LANES), F32)
    return pl.pallas_call(
        functools.partial(_attn_p_kernel, topk=topk, lam_init=lam_init, kt=KT_PROMPT),
        grid=(b, nq),
        in_specs=specs + [full(ta), full(tb),
                          pl.BlockSpec((None, 4, HD_B), lambda bb, i: (l, 0, 0)),
                          pl.BlockSpec((None, 1, LANES), lambda bb, i: (l, 0, 0))],
        out_specs=pl.BlockSpec((None, BLK, 1024), lambda bb, i: (bb, i, 0)),
        out_shape=jax.ShapeDtypeStruct((b, t, 1024), BF16),
        scratch_shapes=[pltpu.VMEM((nq, BLK, BLK), I32), pltpu.VMEM((nq, BLK, BLK), F32),
                        col(NPA, 2 * BLK), col(NPA, 2 * BLK), acc(NPA, 2 * BLK),
                        col(NPB, 4 * BLK), col(NPB, 4 * BLK), acc(NPB, 4 * BLK),
                        col(NPC, 2 * BLK), acc(NPC, 2 * BLK)],
        compiler_params=pltpu.CompilerParams(dimension_semantics=("arbitrary", "arbitrary"),
                                             vmem_limit_bytes=VMEM_LIMIT),
        name="attn_prompt",
    )(*args, ta, tb, dl, g2)


def _sel_s_kernel(pt_ref, qi_ref, wi_ref, kn_ref, *rest, g, n_pages, topk):
    page_refs = rest[:g]
    o_ref = rest[g]
    keys_ref, wide_ref = rest[g + 1], rest[g + 2]
    j = pl.program_id(1)
    ns = n_pages // g
    q = qi_ref[...]
    wi = wi_ref[...]
    kk = jnp.concatenate([page_refs[r][...].astype(BF16) for r in range(g)], axis=1)
    keys_ref[j] = _sortable(_stacked_index_scores(q, kk, wi, SQ))

    @pl.when(j == ns - 1)
    def _():
        t_loc = lax.broadcasted_iota(I32, (SQ, BLK), 0)
        s_loc = lax.broadcasted_iota(I32, (SQ, BLK), 1)
        new_ok = s_loc <= t_loc
        for jj in range(ns):
            wide_ref[:, jj * g * BLK:(jj + 1) * g * BLK] = keys_ref[jj]
        score = _stacked_index_scores(q, kn_ref[...], wi, SQ)
        wide_ref[:, n_pages * BLK:] = jnp.where(new_ok, _sortable(score), INT_MIN)
        o_ref[...] = _topk_select_wide(wide_ref, [None] * n_pages + [new_ok], topk, SQ)


def _select_sample(page_table, qi_stack, wi, ki_new, pool_kidx, l, g, topk):
    nb, n_pages = page_table.shape
    ns = n_pages // g
    width = (n_pages + 1) * BLK

    def page_spec(r):
        return pl.BlockSpec((None, None, IDX_DIM, BLK),
                            lambda bb, j, pt, r=r: (l, pt[bb, j * g + r], 0, 0))

    grid_spec = pltpu.PrefetchScalarGridSpec(
        num_scalar_prefetch=1,
        grid=(nb, ns),
        in_specs=[pl.BlockSpec((None, N_IDX_HEADS * SQ, IDX_DIM), lambda bb, j, pt: (bb, 0, 0)),
                  pl.BlockSpec((None, SQ, LANES), lambda bb, j, pt: (bb, 0, 0)),
                  pl.BlockSpec((None, IDX_DIM, BLK), lambda bb, j, pt: (bb, 0, 0))]
                 + [page_spec(r) for r in range(g)],
        out_specs=pl.BlockSpec((None, SQ, width), lambda bb, j, pt: (bb, 0, 0)),
        scratch_shapes=[pltpu.VMEM((ns, SQ, g * BLK), I32), pltpu.VMEM((SQ, width), I32)],
    )
    return pl.pallas_call(
        functools.partial(_sel_s_kernel, g=g, n_pages=n_pages, topk=topk),
        grid_spec=grid_spec,
        out_shape=jax.ShapeDtypeStruct((nb, SQ, width), F32),
        compiler_params=pltpu.CompilerParams(dimension_semantics=("arbitrary", "arbitrary"),
                                             vmem_limit_bytes=VMEM_LIMIT),
        name="select_sample",
    )(page_table, qi_stack, wi, ki_new, *([pool_kidx] * g))


def _attn_s_kernel(pt_ref, qa_ref, qd_ref, qc_ref, kan_ref, van_ref, kdn_ref, vdn_ref, kcn_ref,
                   vcn_ref, mnew_ref, mask_ref, ta_ref, tb_ref, dl_ref, g2_ref, *rest,
                   g, n_pages, lam_init):
    pools = [rest[k * g:(k + 1) * g] for k in range(6)]
    o_ref = rest[6 * g]
    ma, la, acca, mb, lb, accb, cc, accc = rest[6 * g + 1:]
    j = pl.program_id(1)
    ns = pl.num_programs(1)
    tq = SQ
    later_mat = _later_matrix()
    qa2 = [_stack_q(qa_ref[:, _blk(p)], 2) for p in range(NPA)]
    qd4 = [_stack_q(qd_ref[:, _blk(u)], 4) for u in range(NPB)]
    qc2 = [_stack_q(qc_ref[:, _blk(p)], 2) for p in range(NPC)]

    def twice(x):
        return jnp.concatenate([x, x], axis=0)

    @pl.when(j == 0)
    def _():
        _softmax_init(ma, la, acca)
        _softmax_init(mb, lb, accb)
        cc[...] = jnp.zeros(cc.shape, F32)
        accc[...] = jnp.zeros(accc.shape, F32)
        am2 = twice(mnew_ref[...])
        for p in range(NPA):
            _softmax_update(ma, la, acca, p, qa2[p], kan_ref[_blk(p), :], van_ref[_blk(p), :],
                            ta_ref[p, 0] + am2, False)
        for u in range(NPB):
            _softmax_update(mb, lb, accb, u, qd4[u], kdn_ref[_blk(u), :], vdn_ref[_blk(u), :],
                            tb_ref[u, 0], False)
        t2 = lax.broadcasted_iota(I32, (2 * tq, BLK), 0) & (tq - 1)
        s2 = lax.broadcasted_iota(I32, (2 * tq, BLK), 1)
        _stick_update(cc, accc, qc2, [kcn_ref[_blk(p), :] for p in range(NPC)],
                      [vcn_ref[_blk(p), :] for p in range(NPC)], s2 < t2, later_mat)

    def pages(kind, rows):
        return jnp.concatenate([pools[kind][r][rows, :].astype(BF16) for r in range(g)], axis=1)

    ty_last = jnp.where(j == 0, 1, 2)

    def bias(ref, idx):
        return jnp.concatenate([ref[idx, 2]] * (g - 1) + [ref[idx, ty_last]], axis=1)

    am2 = twice(mask_ref[...])
    for p in range(NPA):
        _softmax_update(ma, la, acca, p, qa2[p], pages(0, _blk(p)), pages(1, _blk(p)),
                        bias(ta_ref, p) + am2, False)
    for u in range(NPB):
        _softmax_update(mb, lb, accb, u, qd4[u], pages(2, _blk(u)), pages(3, _blk(u)),
                        bias(tb_ref, u), False)
    _stick_update(cc, accc, qc2, [pages(4, _blk(p)) for p in range(NPC)],
                  [pages(5, _blk(p)) for p in range(NPC)], None, later_mat)

    @pl.when(j == ns - 1)
    def _():
        lam = _lambda(dl_ref, lam_init)
        g2 = g2_ref[...]
        for p in range(NPA):
            o_ref[:, _blk(p)] = _pair_out(acca[p], la[p], tq).astype(o_ref.dtype)
        for u in range(NPB):
            o_ref[:, _blk(NPA + u)] = _diff_out(accb[u], lb[u], tq, lam, g2,
                                               1.0 - lam_init).astype(o_ref.dtype)
        for p in range(NPC):
            acc = accc[p]
            o_ref[:, _blk(NPA + NPB + p)] = _sel2(acc[:tq], acc[tq:]).astype(o_ref.dtype)


def _attn_sample(page_table, q_arrays, new_arrays, addmask, ta, tb, dl, g2, pools, l, g, lam_init):
    nb, n_pages = page_table.shape
    ns = n_pages // g

    def seq_spec(a):
        return pl.BlockSpec((None,) + a.shape[1:], lambda bb, j, pt: (bb, 0, 0))

    def full(a):
        return pl.BlockSpec(a.shape, lambda bb, j, pt, n=a.ndim: (0,) * n)

    def page_spec(w, r):
        return pl.BlockSpec((None, None, w, BLK),
                            lambda bb, j, pt, r=r: (l, pt[bb, n_pages - (j + 1) * g + r], 0, 0))

    in_specs = [seq_spec(a) for a in q_arrays] + [seq_spec(a) for a in new_arrays]
    in_specs += [pl.BlockSpec((None, SQ, BLK), lambda bb, j, pt: (bb, 0, n_pages)),
                 pl.BlockSpec((None, SQ, g * BLK), lambda bb, j, pt: (bb, 0, ns - 1 - j)),
                 full(ta), full(tb),
                 pl.BlockSpec((None, 4, HD_B), lambda bb, j, pt: (l, 0, 0)),
                 pl.BlockSpec((None, 1, LANES), lambda bb, j, pt: (l, 0, 0))]
    args = list(q_arrays) + list(new_arrays) + [addmask, addmask, ta, tb, dl, g2]
    for pool in pools:
        w = pool.shape[-2]
        in_specs += [page_spec(w, r) for r in range(g)]
        args += [pool] * g
    col = lambda n, r: pltpu.VMEM((n, r, LANES), F32)
    acc = lambda n, r: pltpu.VMEM((n, r, LANES), F32)
    grid_spec = pltpu.PrefetchScalarGridSpec(
        num_scalar_prefetch=1,
        grid=(nb, ns),
        in_specs=in_specs,
        out_specs=pl.BlockSpec((None, SQ, 1024), lambda bb, j, pt: (bb, 0, 0)),
        scratch_shapes=[col(NPA, 2 * SQ), col(NPA, 2 * SQ), acc(NPA, 2 * SQ),
                        col(NPB, 4 * SQ), col(NPB, 4 * SQ), acc(NPB, 4 * SQ),
                        col(NPC, 2 * SQ), acc(NPC, 2 * SQ)],
    )
    return pl.pallas_call(
        functools.partial(_attn_s_kernel, g=g, n_pages=n_pages, lam_init=lam_init),
        grid_spec=grid_spec,
        out_shape=jax.ShapeDtypeStruct((nb, SQ, 1024), BF16),
        compiler_params=pltpu.CompilerParams(dimension_semantics=("arbitrary", "arbitrary"),
                                             vmem_limit_bytes=VMEM_LIMIT),
        name="attn_sample",
    )(page_table, *args)


def _proj_weights(w_in):
    splits = (384, 384, 384, 256, 4, 64, 256, 256, 256, 384, 384, 384)
    offs = np.concatenate([[0], np.cumsum(splits)])
    part = {n: w_in[:, :, offs[k]:offs[k + 1]] for k, n in enumerate(
        ("qa", "ka", "va", "qi", "wi", "ki", "qd", "kd", "vd", "qc", "kc", "vc"))}
    part["qa"] = part["qa"] * HEAD_DIM ** -0.5
    part["qc"] = part["qc"] * HEAD_DIM ** -0.5
    part["qi"] = part["qi"] * IDX_DIM ** -0.5
    part["qd"] = part["qd"] * HD_B ** -0.5
    part["wi"] = jnp.pad(part["wi"] * N_IDX_HEADS ** -0.5, ((0, 0), (0, 0), (0, LANES - N_IDX_HEADS)))
    part["ki2"] = jnp.concatenate([part["ki"], part["ki"]], axis=-1)
    wq = jnp.concatenate([part[n] for n in _QCOLS], axis=-1).astype(BF16)
    wk = jnp.swapaxes(jnp.concatenate([part[n] for n in _KROWS], axis=-1), 1, 2).astype(BF16)
    return wq, wk


def _row_tile(m):
    for tm in (512, 256, 128, 64, 32, 16, 8):
        if m % tm == 0:
            return tm
    raise ValueError(m)


def _feature_major(cache):
    nd = cache.ndim
    c = jnp.transpose(cache, (0, 1) + tuple(range(3, nd)) + (2,))
    return c.reshape(c.shape[0], c.shape[1], -1, c.shape[-1])


def kernel(x_prompt, x_sample, c_prompt, c_sample, cache_a_k, cache_a_v, cache_a_kidx, cache_b_k,
           cache_b_v, cache_c_k, cache_c_v, page_table, rel_bias, w_mod, b_mod, g_attn, g_ffn, w_in,
           diff_lambda, subln_g, w_out, w_ffn_in, w_ffn_out, g_final):
    b, t, d = x_prompt.shape
    nb, ts, _ = x_sample.shape
    depth = w_in.shape[0]
    page = cache_a_k.shape[2]
    n_pages = page_table.shape[1]
    assert page == BLK and t % (KT_PROMPT * BLK) == 0 and ts <= 8 and d == 1024
    g_att = 16 if n_pages % 16 == 0 else (4 if n_pages % 4 == 0 else 1)
    g_sel = 16 if n_pages % 16 == 0 else g_att
    topk_p = min(TOPK_MAX, t // 4)
    topk_s = min(TOPK_MAX, (n_pages * page + ts) // 4)

    wq, wk = _proj_weights(w_in)
    w_out_b = w_out.astype(BF16)
    w_ffn_in_b = w_ffn_in.astype(BF16)
    w_ffn_out_b = w_ffn_out.astype(BF16)
    g_attn3 = g_attn.reshape(depth, 1, d)
    g_ffn3 = g_ffn.reshape(depth, 1, d)
    g_final2 = g_final.reshape(1, d)
    g2 = jnp.concatenate([subln_g, subln_g], axis=-1).reshape(depth, 1, LANES)
    pools = [_feature_major(c) for c in (cache_a_k, cache_a_v, cache_b_k, cache_b_v, cache_c_k, cache_c_v)]
    pool_kidx = _feature_major(cache_a_kidx)

    mod = _modulation(jnp.concatenate([c_prompt, c_sample], axis=0), w_mod, b_mod)
    mod_p = mod[:, :, :b].reshape(depth, 6, b, 1, d)
    mod_s = jnp.repeat(mod[:, :, b:], ts, axis=2).reshape(depth, 6, 1, nb * ts, d)

    tiles = _bias_tiles(rel_bias)

    def stacked(heads, rows):
        return jnp.concatenate([tiles[h][:, :rows] for h in heads], axis=1)

    pairs_a = [(2 * p, 2 * p + 1) for p in range(NPA)]
    quads_b = [(H_A + 2 * u,) * 2 + (H_A + 2 * u + 1,) * 2 for u in range(NPB)]
    ta_p = jnp.stack([stacked(hs, BLK) for hs in pairs_a])
    tb_p = jnp.stack([stacked(hs, BLK) for hs in quads_b])
    ta_s = jnp.stack([stacked(hs, SQ) for hs in pairs_a])
    tb_s = jnp.stack([stacked(hs, SQ) for hs in quads_b])

    xp = x_prompt.reshape(b * t, d)
    xs = x_sample.reshape(nb * ts, d)
    tm_p = _row_tile(t)
    tm_s = nb * ts
    new_p = {k: [] for k in _KV}
    new_s = {k: [] for k in _KV}

    def pad_q(a):
        return jnp.pad(a.reshape(nb, ts, -1), ((0, 0), (0, SQ - ts), (0, 0)))

    def new_keys(a, w=None):
        a = a[0] if w is None else a[0, :w]
        a = jnp.transpose(a.reshape(a.shape[0], nb, ts), (1, 0, 2))
        return jnp.pad(a, ((0, 0), (0, 0), (0, BLK - ts))).astype(BF16)

    for l in range(depth):
        lam_init = 0.8 - 0.6 * math.exp(-0.3 * l)
        final = l == depth - 1
        pr = _in_proj(xp, g_attn3, mod_p, wq, wk, l, b, tm_p, t // tm_p, True)
        mix = _attn_prompt(pr, ta_p, tb_p, diff_lambda, g2, l, b, t, topk_p, lam_init)
        xp = _out_proj(mix.reshape(b * t, d), xp, mod_p, w_out_b, l, tm_p, t // tm_p)
        xp = _ffn(xp, g_ffn3, mod_p, w_ffn_in_b, w_ffn_out_b, g_final2, l, tm_p, t // tm_p, final)
        sr = _in_proj(xs, g_attn3, mod_s, wq, wk, l, 1, tm_s, None, False)
        qi_stack = pad_q(sr["qi"]).reshape(nb, SQ, N_IDX_HEADS, IDX_DIM).transpose(0, 2, 1, 3)
        qi_stack = qi_stack.reshape(nb, N_IDX_HEADS * SQ, IDX_DIM)
        addmask = _select_sample(page_table, qi_stack, pad_q(sr["wi"]), new_keys(sr["ki"]),
                                 pool_kidx, l, g_sel, topk_s)
        mix_s = _attn_sample(page_table,
                             [pad_q(sr["qa"]), pad_q(sr["qd"]), pad_q(sr["qc"])],
                             [new_keys(sr[n]) for n in ("ka", "va", "kd", "vd", "kc", "vc")],
                             addmask, ta_s, tb_s, diff_lambda, g2, pools, l, g_att, lam_init)
        mix_s = mix_s[:, :ts].reshape(nb * ts, d)
        xs = _out_proj(mix_s, xs, mod_s, w_out_b, l, tm_s, None)
        xs = _ffn(xs, g_ffn3, mod_s, w_ffn_in_b, w_ffn_out_b, g_final2, l, tm_s, None, final)
        for k in _KV:
            new_p[k].append(pr[k])
            new_s[k].append(sr[k])

    tails = {"ka": (H_A, HEAD_DIM), "va": (H_A, HEAD_DIM), "ki": (IDX_DIM,), "kd": (H_B, 2, HD_B),
             "vd": (H_B, 2 * HD_B), "kc": (H_C, HEAD_DIM), "vc": (H_C, HEAD_DIM)}

    def rows_out(arrs, lead, tail):
        a = jnp.stack(arrs)
        nbt, tt = a.shape[1], a.shape[3]
        a = a.reshape((depth, nbt) + tail + (tt,))
        nd = a.ndim
        a = jnp.transpose(a, (0, 1, nd - 1) + tuple(range(2, nd - 1)))
        return a.reshape((depth,) + lead + tail)

    outs_p = [rows_out(new_p[k], (b, t), tails[k]) for k in _KV]
    outs_s = [rows_out(new_s[k], (nb, ts), tails[k]) for k in _KV]
    return (xp.reshape(b, t, d), xs.reshape(nb, ts, d), *outs_p, *outs_s)
```

```python
import functools
import math

import numpy as np
import jax
import jax.numpy as jnp
from jax import lax
from jax.experimental import pallas as pl
from jax.experimental.pallas import tpu as pltpu

F32, BF16, I32 = jnp.float32, jnp.bfloat16, jnp.int32

HEAD_DIM = 64
H_A, H_B, H_C = 6, 4, 6
HD_B = HEAD_DIM // 2
N_IDX_HEADS = 4
IDX_DIM = 64
TOPK_MAX = 256
NUM_BUCKETS = 32
MAX_DISTANCE = 128
EPS = 1e-6
NEG = -1e30
M_INIT = -3e38
INT_MIN = -2 ** 31
LANES = 128
BLK = 128
SQ = 16
KT_PROMPT = 4
VMEM_LIMIT = 56 * 1024 * 1024
NPA, NPB, NPC = H_A // 2, H_B // 2, H_C // 2


def _layout(groups):
    out, off = {}, 0
    for n, w in groups:
        out[n] = (off, w)
        off += w
    return out, off


_QCOLS, N_Q = _layout((("qa", 384), ("qi", 256), ("wi", 128), ("qd", 256), ("qc", 384)))
_KROWS, N_K = _layout((("ka", 384), ("va", 384), ("ki2", 128), ("kd", 256), ("vd", 256),
                       ("kc", 384), ("vc", 384)))
_KV = ("ka", "va", "ki", "kd", "vd", "kc", "vc")


def _krow(name):
    return (_KROWS["ki2"][0], IDX_DIM) if name == "ki" else _KROWS[name]


def _bucket_thresholds():
    max_exact = NUM_BUCKETS // 2
    n = np.arange(max_exact, 4 * MAX_DISTANCE, dtype=np.float32)
    large = max_exact + (np.log(n / max_exact) / math.log(MAX_DISTANCE / max_exact)
                         * (NUM_BUCKETS - max_exact)).astype(np.int32)
    large = np.minimum(large, NUM_BUCKETS - 1)
    return [int(n[np.argmax(large >= b)]) for b in range(max_exact + 1, NUM_BUCKETS)]


_THRS = _bucket_thresholds()


def _mm(a, b):
    return lax.dot_general(a.astype(BF16), b.astype(BF16), (((1,), (0,)), ((), ())),
                           preferred_element_type=F32)


def _mm_nt(a, b):
    return lax.dot_general(a.astype(BF16), b.astype(BF16), (((1,), (1,)), ((), ())),
                           preferred_element_type=F32)


def _split(a):
    hi = a.astype(BF16)
    lo = (a - hi.astype(F32)).astype(BF16)
    return hi, lo


def _mm3(a, b):
    ah, al = _split(a)
    bh, bl = _split(b)
    return _mm(ah, bh) + (_mm(ah, bl) + _mm(al, bh))


def _sigmoid(x):
    return 1.0 / (1.0 + jnp.exp(-x))


def _rmsnorm(x, g):
    return x * lax.rsqrt(jnp.mean(x * x, axis=-1, keepdims=True) + EPS) * g


def _lane():
    return lax.broadcasted_iota(I32, (1, LANES), 1)


def _stack_q(q, parts):
    grp = _lane() >> int(math.log2(LANES // parts))
    zero = jnp.zeros_like(q)
    return jnp.concatenate([jnp.where(grp == r, q, zero) for r in range(parts)], axis=0)


def _sel2(lo_val, hi_val):
    return jnp.where(_lane() < HEAD_DIM, lo_val, hi_val)


def _rowsum(x):
    return jnp.sum(x, axis=-1, keepdims=True)


def _tile_lanes(x, n):
    return x if n == 1 else jnp.concatenate([x] * n, axis=1)


def _softmax_update(m_ref, l_ref, acc_ref, idx, q_st, kk, vv, add, mxu_rowsum):
    s = _mm(q_st, kk) + add
    m_old = m_ref[idx]
    m_new = jnp.maximum(m_old, jnp.max(s, axis=-1, keepdims=True))
    p = jnp.exp(s - _tile_lanes(m_new, s.shape[1] // LANES)).astype(BF16)
    alpha = jnp.exp(m_old - m_new)
    if mxu_rowsum:
        both = _mm_nt(p, jnp.concatenate([vv.astype(BF16), jnp.ones(vv.shape, BF16)], axis=0))
        pv, psum = both[:, :LANES], both[:, LANES:]
    else:
        pv, psum = _mm_nt(p, vv), _rowsum(p.astype(F32))
    l_ref[idx] = alpha * l_ref[idx] + psum
    acc_ref[idx] = alpha * acc_ref[idx] + pv
    m_ref[idx] = m_new


def _softmax_init(m_ref, l_ref, acc_ref):
    m_ref[...] = jnp.full(m_ref.shape, M_INIT, F32)
    l_ref[...] = jnp.zeros(l_ref.shape, F32)
    acc_ref[...] = jnp.zeros(acc_ref.shape, F32)


def _pair_out(acc, l, tq):
    o = acc / l
    return _sel2(o[:tq], o[tq:])


def _diff_out(acc, l, tq, lam, g2, out_scale):
    o = acc / l
    o = _sel2(o[:tq], o[2 * tq:3 * tq]) - lam * _sel2(o[tq:2 * tq], o[3 * tq:])
    o2 = o * o
    lo = _lane() < HEAD_DIM
    ms_lo = _rowsum(jnp.where(lo, o2, 0.0)) * (1.0 / HEAD_DIM)
    ms_hi = _rowsum(jnp.where(lo, 0.0, o2)) * (1.0 / HEAD_DIM)
    r = _sel2(lax.rsqrt(ms_lo + EPS), lax.rsqrt(ms_hi + EPS))
    return o * r * g2 * out_scale


def _later_matrix():
    j = lax.broadcasted_iota(I32, (BLK, 2 * BLK), 0)
    s = lax.broadcasted_iota(I32, (BLK, 2 * BLK), 1)
    return ((j > s) | (s >= BLK)).astype(BF16)


def _stick_update(c_ref, acc_ref, q_sts, kks, vvs, strict, later_mat):
    n = kks[0].shape[1] // BLK
    rows = q_sts[0].shape[0]
    lbs, lks = [], []
    for q_st, kk in zip(q_sts, kks):
        z = _mm(q_st, kk)
        lb = jnp.minimum(z, 0.0) - jnp.log(1.0 + jnp.exp(-jnp.abs(z)))
        lk = lb - z
        if strict is not None:
            lk = jnp.where(strict, lk, 0.0)
        lk = lk.astype(BF16)
        lbs.append(lb)
        lks += [lk[:, r * BLK:(r + 1) * BLK] for r in range(n)]
    both = _mm(jnp.concatenate(lks, axis=0), later_mat)
    for idx, (lb, vv) in enumerate(zip(lbs, vvs)):
        c = c_ref[idx]
        laters = [None] * n
        for r in reversed(range(n)):
            part = both[(idx * n + r) * rows:(idx * n + r + 1) * rows]
            laters[r] = part[:, :BLK] + c
            c = c + part[:, BLK:]
        later = laters[0] if n == 1 else jnp.concatenate(laters, axis=1)
        a = jnp.exp(lb + later)
        if strict is not None:
            a = jnp.where(strict, a, 0.0)
        acc_ref[idx] = acc_ref[idx] + _mm_nt(a, vv)
        c_ref[idx] = c


def _sortable(score):
    i = lax.bitcast_convert_type(score + 0.0, I32)
    return i ^ ((i >> 31) & 0x7FFFFFFF)


def _before_matrix():
    jj = lax.broadcasted_iota(I32, (BLK, BLK), 0)
    ss = lax.broadcasted_iota(I32, (BLK, BLK), 1)
    return (jj < ss).astype(BF16)


def _kth_largest(count_ge, tq, topk):
    kf = float(topk)
    zero = jnp.zeros((tq, 1), I32)
    v = jnp.where(count_ge(zero) >= kf, zero, jnp.full((tq, 1), INT_MIN, I32))

    def bit_body(bi, v):
        cand = v | lax.shift_left(jnp.int32(1), 30 - bi)
        return jnp.where(count_ge(cand) >= kf, cand, v)

    return lax.fori_loop(0, 31, bit_body, v)


def _topk_select_blocks(keys_ref, mask_ref, ngrp, grp, topk, tq, valid_fn):
    def count(pred):
        def body(gi, acc):
            for c in range(grp):
                acc = acc + pred(keys_ref[gi * grp + c]).astype(F32)
            return acc
        return _rowsum(lax.fori_loop(0, ngrp, body, jnp.zeros((tq, LANES), F32)))

    v = _kth_largest(lambda cand: count(lambda key: key >= cand), tq, topk)
    need = float(topk) - count(lambda key: key > v)
    before = _before_matrix()

    def sel_body(gi, run):
        keys = [keys_ref[gi * grp + c] for c in range(grp)]
        eqs = [(key == v).astype(F32) for key in keys]
        ranks = _mm(jnp.concatenate(eqs, axis=0), before)
        for c, (key, eqf) in enumerate(zip(keys, eqs)):
            kb = gi * grp + c
            rank = ranks[c * tq:(c + 1) * tq] + run
            sel = (key > v) | ((key == v) & (rank < need))
            mask_ref[kb] = jnp.where(sel & valid_fn(kb), 0.0, NEG)
            run = run + _rowsum(eqf)
        return run

    lax.fori_loop(0, ngrp, sel_body, jnp.zeros((tq, 1), F32))


def _topk_select_wide(keys_ref, valid, topk, tq):
    def count(pred):
        return _rowsum(pred(keys_ref[...]).astype(F32))

    v = _kth_largest(lambda cand: count(lambda key: key >= cand), tq, topk)
    need = float(topk) - count(lambda key: key > v)
    before = _before_matrix()
    run = jnp.zeros((tq, 1), F32)
    masks = []
    for kb in range(keys_ref.shape[1] // BLK):
        sl = slice(kb * BLK, (kb + 1) * BLK)
        key = keys_ref[:, sl]
        eqf = (key == v).astype(F32)
        rank = _mm(eqf, before) + run
        sel = (key > v) | ((key == v) & (rank < need))
        if valid[kb] is not None:
            sel = sel & valid[kb]
        masks.append(jnp.where(sel, 0.0, NEG))
        run = run + _rowsum(eqf)
    return jnp.concatenate(masks, axis=1)


def _stacked_index_scores(q_stack, kk, wi, tq):
    s = jnp.maximum(_mm(q_stack, kk), 0.0)
    score = None
    for n in range(N_IDX_HEADS):
        term = s[n * tq:(n + 1) * tq] * wi[:, n:n + 1]
        score = term if score is None else score + term
    return score


def _mod_kernel(c_ref, w_ref, b_ref, o_ref):
    c = c_ref[...]
    o_ref[...] = _mm3(c * _sigmoid(c), w_ref[...]) + b_ref[...]


def _modulation(c_all, w_mod, b_mod):
    depth, d, _ = w_mod.shape
    n = c_all.shape[0]
    return pl.pallas_call(
        _mod_kernel,
        grid=(depth, 6),
        in_specs=[pl.BlockSpec((n, d), lambda l, c: (0, 0)),
                  pl.BlockSpec((None, d, d), lambda l, c: (l, 0, c)),
                  pl.BlockSpec((None, None, 1, d), lambda l, c: (l, c, 0, 0))],
        out_specs=pl.BlockSpec((None, None, n, d), lambda l, c: (l, c, 0, 0)),
        out_shape=jax.ShapeDtypeStruct((depth, 6, n, d), F32),
        compiler_params=pltpu.CompilerParams(dimension_semantics=("arbitrary", "arbitrary"),
                                             vmem_limit_bytes=VMEM_LIMIT),
        name="modulation",
    )(c_all, w_mod, b_mod.reshape(depth, 6, 1, d))


def _bias_kernel(rb_ref, o_ref):
    h = pl.program_id(0)
    t = lax.broadcasted_iota(I32, (BLK, BLK), 0)
    s = lax.broadcasted_iota(I32, (BLK, BLK), 1)
    for ty, c in ((0, 0), (1, BLK)):
        rel = c + t - s
        n = jnp.maximum(rel, 0)
        large = jnp.full((BLK, BLK), NUM_BUCKETS // 2, I32)
        for thr in _THRS:
            large = large + (n >= thr).astype(I32)
        bucket = jnp.where(n < NUM_BUCKETS // 2, n, large)
        val = jnp.zeros((BLK, BLK), F32)
        for b in range(NUM_BUCKETS):
            val = jnp.where(bucket == b, rb_ref[b, h], val)
        if ty == 0:
            val = jnp.where(rel >= 0, val, NEG)
        o_ref[ty] = val
    o_ref[2] = jnp.full((BLK, BLK), rb_ref[NUM_BUCKETS - 1, h], F32)
    o_ref[3] = jnp.full((BLK, BLK), NEG, F32)


def _bias_tiles(rel_bias):
    nh = rel_bias.shape[1]
    return pl.pallas_call(
        _bias_kernel,
        grid=(nh,),
        in_specs=[pl.BlockSpec(memory_space=pltpu.SMEM)],
        out_specs=pl.BlockSpec((None, 4, BLK, BLK), lambda h: (h, 0, 0, 0)),
        out_shape=jax.ShapeDtypeStruct((nh, 4, BLK, BLK), F32),
        name="bias_tiles",
    )(rel_bias)


_Q_OUTS = (("qa", BF16), ("qi", BF16), ("wi", F32), ("qd", BF16), ("qc", BF16))
_KB_OUTS = ("ka", "va", "ki2", "kd", "vd", "kc", "vc")


def _in_kernel(x_ref, g_ref, sh_ref, sc_ref, wq_ref, wk_ref, *outs, tiled):
    x = x_ref[...]
    h = (_rmsnorm(x, g_ref[...]) * (1.0 + sc_ref[...]) + sh_ref[...]).astype(BF16)
    yq = _mm(h, wq_ref[...])
    yk = _mm_nt(wk_ref[...], h)
    outs = list(outs)
    for name, dt in _Q_OUTS:
        off, w = _QCOLS[name]
        outs.pop(0)[...] = yq[:, off:off + w].astype(dt)
    for name in _KV:
        off, w = _krow(name)
        outs.pop(0)[...] = yk[off:off + w, :]
    if tiled:
        for name in _KB_OUTS:
            off, w = _KROWS[name]
            o = outs.pop(0)
            for c in range(o.shape[0]):
                o[c] = yk[off:off + w, c * BLK:(c + 1) * BLK].astype(BF16)


def _mod_spec(tm, d, rows_per_mod, l, c):
    if rows_per_mod is None:
        return pl.BlockSpec((None, None, None, tm, d), lambda i: (l, c, 0, i, 0))
    return pl.BlockSpec((None, None, None, 1, d), lambda i: (l, c, i // rows_per_mod, 0, 0))


def _in_proj(x, g, mod, wq, wk, l, nbatch, tm, tiles_per_mod, tiled):
    m, d = x.shape
    t = m // nbatch
    tpb = t // tm
    out_shapes, out_specs = [], []
    for name, dt in _Q_OUTS:
        w = _QCOLS[name][1]
        out_shapes.append(jax.ShapeDtypeStruct((m, w), dt))
        out_specs.append(pl.BlockSpec((tm, w), lambda i: (i, 0)))
    for name in _KV:
        w = _krow(name)[1]
        out_shapes.append(jax.ShapeDtypeStruct((nbatch, w, t), F32))
        out_specs.append(pl.BlockSpec((None, w, tm), lambda i: (i // tpb, 0, i % tpb)))
    if tiled:
        for name in _KB_OUTS:
            w = _KROWS[name][1]
            out_shapes.append(jax.ShapeDtypeStruct((nbatch, t // BLK, w, BLK), BF16))
            out_specs.append(pl.BlockSpec((None, tm // BLK, w, BLK), lambda i: (i // tpb, i % tpb, 0, 0)))
    outs = pl.pallas_call(
        functools.partial(_in_kernel, tiled=tiled),
        grid=(m // tm,),
        in_specs=[pl.BlockSpec((tm, d), lambda i: (i, 0)),
                  pl.BlockSpec((None, 1, d), lambda i: (l, 0, 0)),
                  _mod_spec(tm, d, tiles_per_mod, l, 0),
                  _mod_spec(tm, d, tiles_per_mod, l, 1),
                  pl.BlockSpec((None, d, N_Q), lambda i: (l, 0, 0)),
                  pl.BlockSpec((None, N_K, d), lambda i: (l, 0, 0))],
        out_specs=out_specs,
        out_shape=out_shapes,
        compiler_params=pltpu.CompilerParams(dimension_semantics=("arbitrary",),
                                             vmem_limit_bytes=VMEM_LIMIT),
        name="in_proj",
    )(x, g, mod, mod, wq, wk)
    names = [n for n, _ in _Q_OUTS] + list(_KV) + ([n + "_t" for n in _KB_OUTS] if tiled else [])
    return dict(zip(names, outs))


def _out_kernel(mix_ref, x_ref, gt_ref, w_ref, o_ref):
    o_ref[...] = x_ref[...] + gt_ref[...] * _mm(mix_ref[...], w_ref[...])


def _out_proj(mix, x, mod, w_out, l, tm, tiles_per_mod):
    m, d = x.shape
    return pl.pallas_call(
        _out_kernel,
        grid=(m // tm,),
        in_specs=[pl.BlockSpec((tm, d), lambda i: (i, 0)),
                  pl.BlockSpec((tm, d), lambda i: (i, 0)),
                  _mod_spec(tm, d, tiles_per_mod, l, 2),
                  pl.BlockSpec((None, d, d), lambda i: (l, 0, 0))],
        out_specs=pl.BlockSpec((tm, d), lambda i: (i, 0)),
        out_shape=jax.ShapeDtypeStruct((m, d), F32),
        compiler_params=pltpu.CompilerParams(dimension_semantics=("arbitrary",),
                                             vmem_limit_bytes=VMEM_LIMIT),
        name="out_proj",
    )(mix, x, mod, w_out)


def _ffn_kernel(x_ref, g_ref, sh_ref, sc_ref, gt_ref, wg_ref, wu_ref, wo_ref, gf_ref, o_ref,
                h_ref, acc_ref, *, final):
    j = pl.program_id(1)

    @pl.when(j == 0)
    def _():
        h = _rmsnorm(x_ref[...], g_ref[...]) * (1.0 + sc_ref[...]) + sh_ref[...]
        h_ref[...] = h.astype(BF16)
        acc_ref[...] = jnp.zeros_like(acc_ref)

    h = h_ref[...]
    gate = _mm(h, wg_ref[...])
    up = _mm(h, wu_ref[...])
    acc_ref[...] += _mm(gate * _sigmoid(gate) * up, wo_ref[...])

    @pl.when(j == pl.num_programs(1) - 1)
    def _():
        y = x_ref[...] + gt_ref[...] * acc_ref[...]
        if final:
            y = _rmsnorm(y, gf_ref[...])
        o_ref[...] = y


def _ffn(x, g, mod, w_in, w_out, g_final, l, tm, tiles_per_mod, final):
    m, d = x.shape
    dff = w_out.shape[1]
    nf = 2 if dff % (2 * LANES) == 0 else 1
    tf = dff // nf

    def ms(c):
        spec = _mod_spec(tm, d, tiles_per_mod, l, c)
        return pl.BlockSpec(spec.block_shape, lambda i, j, f=spec.index_map: f(i))

    return pl.pallas_call(
        functools.partial(_ffn_kernel, final=final),
        grid=(m // tm, nf),
        in_specs=[pl.BlockSpec((tm, d), lambda i, j: (i, 0)),
                  pl.BlockSpec((None, 1, d), lambda i, j: (l, 0, 0)),
                  ms(3), ms(4), ms(5),
                  pl.BlockSpec((None, d, tf), lambda i, j: (l, 0, j)),
                  pl.BlockSpec((None, d, tf), lambda i, j: (l, 0, nf + j)),
                  pl.BlockSpec((None, tf, d), lambda i, j: (l, j, 0)),
                  pl.BlockSpec((1, d), lambda i, j: (0, 0))],
        out_specs=pl.BlockSpec((tm, d), lambda i, j: (i, 0)),
        out_shape=jax.ShapeDtypeStruct((m, d), F32),
        scratch_shapes=[pltpu.VMEM((tm, d), BF16), pltpu.VMEM((tm, d), F32)],
        compiler_params=pltpu.CompilerParams(dimension_semantics=("arbitrary", "arbitrary"),
                                             vmem_limit_bytes=VMEM_LIMIT),
        name="ffn",
    )(x, g, mod, mod, mod, w_in, w_in, w_out, g_final)


def _lambda(dl_ref, lam_init):
    dl = dl_ref[...]
    a = jnp.sum(dl[0:1] * dl[1:2], axis=(0, 1), keepdims=True)
    b = jnp.sum(dl[2:3] * dl[3:4], axis=(0, 1), keepdims=True)
    return jnp.exp(a) - jnp.exp(b) + lam_init


def _blk(p):
    return slice(p * LANES, (p + 1) * LANES)


def _attn_p_kernel(qa_ref, ka_ref, va_ref, qi_ref, wi_ref, ki2_ref, qd_ref, kd_ref, vd_ref,
                   qc_ref, kc_ref, vc_ref, ta_ref, tb_ref, dl_ref, g2_ref, o_ref,
                   keys_ref, mask_ref, ma, la, acca, mb, lb, accb, cc, accc, *, topk, lam_init, kt):
    i = pl.program_id(1)
    tq = BLK
    nblk = i + 1
    t_loc = lax.broadcasted_iota(I32, (tq, BLK), 0)
    s_loc = lax.broadcasted_iota(I32, (tq, BLK), 1)

    def causal(kb):
        return kb * BLK + s_loc <= i * BLK + t_loc

    qi2 = [_stack_q(qi_ref[:, _blk(b)], 2) for b in range(N_IDX_HEADS // 2)]
    wi = wi_ref[...]
    wib = [jnp.broadcast_to(wi[:, n:n + 1], (tq, kt * BLK)) for n in range(N_IDX_HEADS)]
    ngrp = i // kt + 1

    def score_body(gi, carry):
        kk = jnp.concatenate([ki2_ref[gi * kt + c] for c in range(kt)], axis=1)
        score = None
        for b in range(N_IDX_HEADS // 2):
            s = jnp.maximum(_mm(qi2[b], kk), 0.0)
            term = s[:tq] * wib[2 * b] + s[tq:] * wib[2 * b + 1]
            score = term if score is None else score + term
        key = _sortable(score)
        for c in range(kt):
            kb = gi * kt + c
            keys_ref[kb] = jnp.where(causal(kb), key[:, c * BLK:(c + 1) * BLK], INT_MIN)
        return carry

    lax.fori_loop(0, ngrp, score_body, 0)
    _topk_select_blocks(keys_ref, mask_ref, ngrp, kt, topk, tq, causal)

    qa2 = [_stack_q(qa_ref[:, _blk(p)], 2) for p in range(NPA)]
    qd4 = [_stack_q(qd_ref[:, _blk(u)], 4) for u in range(NPB)]
    qc2 = [_stack_q(qc_ref[:, _blk(p)], 2) for p in range(NPC)]
    later_mat = _later_matrix()
    _softmax_init(ma, la, acca)
    _softmax_init(mb, lb, accb)
    cc[...] = jnp.zeros(cc.shape, F32)
    accc[...] = jnp.zeros(accc.shape, F32)
    t2 = lax.broadcasted_iota(I32, (2 * tq, BLK), 0) & (tq - 1)
    s2 = lax.broadcasted_iota(I32, (2 * tq, BLK), 1)

    def body(r, carry, diagonal):
        first = (i // kt - r) * kt
        kbs = [first + c for c in range(kt)]
        tys = [jnp.where(kb > i, 3, jnp.minimum(i - kb, 2)) for kb in kbs]

        def cat(fn):
            parts = [fn(c) for c in range(kt)]
            return parts[0] if kt == 1 else jnp.concatenate(parts, axis=1)

        def keys(ref, rows):
            return cat(lambda c: ref[kbs[c], rows, :])

        def twice(x):
            return jnp.concatenate([x, x], axis=0)

        am2 = cat(lambda c: twice(mask_ref[jnp.minimum(kbs[c], i)]))
        for p in range(NPA):
            _softmax_update(ma, la, acca, p, qa2[p], keys(ka_ref, _blk(p)), keys(va_ref, _blk(p)),
                            cat(lambda c: ta_ref[p, tys[c]]) + am2, True)
        for u in range(NPB):
            _softmax_update(mb, lb, accb, u, qd4[u], keys(kd_ref, _blk(u)), keys(vd_ref, _blk(u)),
                            cat(lambda c: tb_ref[u, tys[c]]), True)
        strict = cat(lambda c: kbs[c] * BLK + s2 < i * BLK + t2) if diagonal else None
        _stick_update(cc, accc, qc2, [keys(kc_ref, _blk(p)) for p in range(NPC)],
                      [keys(vc_ref, _blk(p)) for p in range(NPC)], strict, later_mat)
        return carry

    body(0, 0, diagonal=True)
    lax.fori_loop(1, i // kt + 1, functools.partial(body, diagonal=False), 0)

    for p in range(NPA):
        o_ref[:, _blk(p)] = _pair_out(acca[p], la[p], tq).astype(o_ref.dtype)
    lam = _lambda(dl_ref, lam_init)
    g2 = g2_ref[...]
    for u in range(NPB):
        o_ref[:, _blk(NPA + u)] = _diff_out(accb[u], lb[u], tq, lam, g2, 1.0 - lam_init).astype(o_ref.dtype)
    for p in range(NPC):
        acc = accc[p]
        o_ref[:, _blk(NPA + NPB + p)] = _sel2(acc[:tq], acc[tq:]).astype(o_ref.dtype)


def _attn_prompt(pr, ta, tb, dl, g2, l, b, t, topk, lam_init):
    nq = t // BLK

    def qspec(w):
        return pl.BlockSpec((None, BLK, w), lambda bb, i: (bb, i, 0))

    def kspec(w):
        return pl.BlockSpec((None, nq, w, BLK), lambda bb, i: (bb, 0, 0, 0))

    def full(a):
        return pl.BlockSpec(a.shape, lambda bb, i, n=a.ndim: (0,) * n)

    r3 = lambda a: a.reshape(b, t, a.shape[-1])
    args = [r3(pr["qa"]), pr["ka_t"], pr["va_t"], r3(pr["qi"]), r3(pr["wi"]), pr["ki2_t"],
            r3(pr["qd"]), pr["kd_t"], pr["vd_t"], r3(pr["qc"]), pr["kc_t"], pr["vc_t"]]
    specs = [qspec(384), kspec(384), kspec(384), qspec(256), qspec(128), kspec(128),
             qspec(256), kspec(256), kspec(256), qspec(384), kspec(384), kspec(384)]
    col = lambda n, r: pltpu.VMEM((n, r, LANES), F32)
    acc = lambda n, r: pltpu.VMEM((n, r, LANES), F32)
    return pl.pallas_call(
        functools.partial(_attn_p_kernel, topk=topk, lam_init=lam_init, kt=KT_PROMPT),
        grid=(b, nq),
        in_specs=specs + [full(ta), full(tb),
                          pl.BlockSpec((None, 4, HD_B), lambda bb, i: (l, 0, 0)),
                          pl.BlockSpec((None, 1, LANES), lambda bb, i: (l, 0, 0))],
        out_specs=pl.BlockSpec((None, BLK, 1024), lambda bb, i: (bb, i, 0)),
        out_shape=jax.ShapeDtypeStruct((b, t, 1024), BF16),
        scratch_shapes=[pltpu.VMEM((nq, BLK, BLK), I32), pltpu.VMEM((nq, BLK, BLK), F32),
                        col(NPA, 2 * BLK), col(NPA, 2 * BLK), acc(NPA, 2 * BLK),
                        col(NPB, 4 * BLK), col(NPB, 4 * BLK), acc(NPB, 4 * BLK),
                        col(NPC, 2 * BLK), acc(NPC, 2 * BLK)],
        compiler_params=pltpu.CompilerParams(dimension_semantics=("arbitrary", "arbitrary"),
                                             vmem_limit_bytes=VMEM_LIMIT),
        name="attn_prompt",
    )(*args, ta, tb, dl, g2)


def _sel_s_kernel(pt_ref, qi_ref, wi_ref, kn_ref, *rest, g, n_pages, topk):
    page_refs = rest[:g]
    o_ref = rest[g]
    keys_ref, wide_ref = rest[g + 1], rest[g + 2]
    j = pl.program_id(1)
    ns = n_pages // g
    q = qi_ref[...]
    wi = wi_ref[...]
    kk = jnp.concatenate([page_refs[r][...].astype(BF16) for r in range(g)], axis=1)
    keys_ref[j] = _sortable(_stacked_index_scores(q, kk, wi, SQ))

    @pl.when(j == ns - 1)
    def _():
        t_loc = lax.broadcasted_iota(I32, (SQ, BLK), 0)
        s_loc = lax.broadcasted_iota(I32, (SQ, BLK), 1)
        new_ok = s_loc <= t_loc
        for jj in range(ns):
            wide_ref[:, jj * g * BLK:(jj + 1) * g * BLK] = keys_ref[jj]
        score = _stacked_index_scores(q, kn_ref[...], wi, SQ)
        wide_ref[:, n_pages * BLK:] = jnp.where(new_ok, _sortable(score), INT_MIN)
        o_ref[...] = _topk_select_wide(wide_ref, [None] * n_pages + [new_ok], topk, SQ)


def _select_sample(page_table, qi_stack, wi, ki_new, pool_kidx, l, g, topk):
    nb, n_pages = page_table.shape
    ns = n_pages // g
    width = (n_pages + 1) * BLK

    def page_spec(r):
        return pl.BlockSpec((None, None, IDX_DIM, BLK),
                            lambda bb, j, pt, r=r: (l, pt[bb, j * g + r], 0, 0))

    grid_spec = pltpu.PrefetchScalarGridSpec(
        num_scalar_prefetch=1,
        grid=(nb, ns),
        in_specs=[pl.BlockSpec((None, N_IDX_HEADS * SQ, IDX_DIM), lambda bb, j, pt: (bb, 0, 0)),
                  pl.BlockSpec((None, SQ, LANES), lambda bb, j, pt: (bb, 0, 0)),
                  pl.BlockSpec((None, IDX_DIM, BLK), lambda bb, j, pt: (bb, 0, 0))]
                 + [page_spec(r) for r in range(g)],
        out_specs=pl.BlockSpec((None, SQ, width), lambda bb, j, pt: (bb, 0, 0)),
        scratch_shapes=[pltpu.VMEM((ns, SQ, g * BLK), I32), pltpu.VMEM((SQ, width), I32)],
    )
    return pl.pallas_call(
        functools.partial(_sel_s_kernel, g=g, n_pages=n_pages, topk=topk),
        grid_spec=grid_spec,
        out_shape=jax.ShapeDtypeStruct((nb, SQ, width), F32),
        compiler_params=pltpu.CompilerParams(dimension_semantics=("arbitrary", "arbitrary"),
                                             vmem_limit_bytes=VMEM_LIMIT),
        name="select_sample",
    )(page_table, qi_stack, wi, ki_new, *([pool_kidx] * g))


def _attn_s_kernel(pt_ref, qa_ref, qd_ref, qc_ref, kan_ref, van_ref, kdn_ref, vdn_ref, kcn_ref,
                   vcn_ref, mnew_ref, mask_ref, ta_ref, tb_ref, dl_ref, g2_ref, *rest,
                   g, n_pages, lam_init):
    pools = [rest[k * g:(k + 1) * g] for k in range(6)]
    o_ref = rest[6 * g]
    ma, la, acca, mb, lb, accb, cc, accc = rest[6 * g + 1:]
    j = pl.program_id(1)
    ns = pl.num_programs(1)
    tq = SQ
    later_mat = _later_matrix()
    qa2 = [_stack_q(qa_ref[:, _blk(p)], 2) for p in range(NPA)]
    qd4 = [_stack_q(qd_ref[:, _blk(u)], 4) for u in range(NPB)]
    qc2 = [_stack_q(qc_ref[:, _blk(p)], 2) for p in range(NPC)]

    def twice(x):
        return jnp.concatenate([x, x], axis=0)

    @pl.when(j == 0)
    def _():
        _softmax_init(ma, la, acca)
        _softmax_init(mb, lb, accb)
        cc[...] = jnp.zeros(cc.shape, F32)
        accc[...] = jnp.zeros(accc.shape, F32)
        am2 = twice(mnew_ref[...])
        for p in range(NPA):
            _softmax_update(ma, la, acca, p, qa2[p], kan_ref[_blk(p), :], van_ref[_blk(p), :],
                            ta_ref[p, 0] + am2, False)
        for u in range(NPB):
            _softmax_update(mb, lb, accb, u, qd4[u], kdn_ref[_blk(u), :], vdn_ref[_blk(u), :],
                            tb_ref[u, 0], False)
        t2 = lax.broadcasted_iota(I32, (2 * tq, BLK), 0) & (tq - 1)
        s2 = lax.broadcasted_iota(I32, (2 * tq, BLK), 1)
        _stick_update(cc, accc, qc2, [kcn_ref[_blk(p), :] for p in range(NPC)],
                      [vcn_ref[_blk(p), :] for p in range(NPC)], s2 < t2, later_mat)

    def pages(kind, rows):
        return jnp.concatenate([pools[kind][r][rows, :].astype(BF16) for r in range(g)], axis=1)

    ty_last = jnp.where(j == 0, 1, 2)

    def bias(ref, idx):
        return jnp.concatenate([ref[idx, 2]] * (g - 1) + [ref[idx, ty_last]], axis=1)

    am2 = twice(mask_ref[...])
    for p in range(NPA):
        _softmax_update(ma, la, acca, p, qa2[p], pages(0, _blk(p)), pages(1, _blk(p)),
                        bias(ta_ref, p) + am2, False)
    for u in range(NPB):
        _softmax_update(mb, lb, accb, u, qd4[u], pages(2, _blk(u)), pages(3, _blk(u)),
                        bias(tb_ref, u), False)
    _stick_update(cc, accc, qc2, [pages(4, _blk(p)) for p in range(NPC)],
                  [pages(5, _blk(p)) for p in range(NPC)], None, later_mat)

    @pl.when(j == ns - 1)
    def _():
        lam = _lambda(dl_ref, lam_init)
        g2 = g2_ref[...]
        for p in range(NPA):
            o_ref[:, _blk(p)] = _pair_out(acca[p], la[p], tq).astype(o_ref.dtype)
        for u in range(NPB):
            o_ref[:, _blk(NPA + u)] = _diff_out(accb[u], lb[u], tq, lam, g2,
                                               1.0 - lam_init).astype(o_ref.dtype)
        for p in range(NPC):
            acc = accc[p]
            o_ref[:, _blk(NPA + NPB + p)] = _sel2(acc[:tq], acc[tq:]).astype(o_ref.dtype)


def _attn_sample(page_table, q_arrays, new_arrays, addmask, ta, tb, dl, g2, pools, l, g, lam_init):
    nb, n_pages = page_table.shape
    ns = n_pages // g

    def seq_spec(a):
        return pl.BlockSpec((None,) + a.shape[1:], lambda bb, j, pt: (bb, 0, 0))

    def full(a):
        return pl.BlockSpec(a.shape, lambda bb, j, pt, n=a.ndim: (0,) * n)

    def page_spec(w, r):
        return pl.BlockSpec((None, None, w, BLK),
                            lambda bb, j, pt, r=r: (l, pt[bb, n_pages - (j + 1) * g + r], 0, 0))

    in_specs = [seq_spec(a) for a in q_arrays] + [seq_spec(a) for a in new_arrays]
    in_specs += [pl.BlockSpec((None, SQ, BLK), lambda bb, j, pt: (bb, 0, n_pages)),
                 pl.BlockSpec((None, SQ, g * BLK), lambda bb, j, pt: (bb, 0, ns - 1 - j)),
                 full(ta), full(tb),
                 pl.BlockSpec((None, 4, HD_B), lambda bb, j, pt: (l, 0, 0)),
                 pl.BlockSpec((None, 1, LANES), lambda bb, j, pt: (l, 0, 0))]
    args = list(q_arrays) + list(new_arrays) + [addmask, addmask, ta, tb, dl, g2]
    for pool in pools:
        w = pool.shape[-2]
        in_specs += [page_spec(w, r) for r in range(g)]
        args += [pool] * g
    col = lambda n, r: pltpu.VMEM((n, r, LANES), F32)
    acc = lambda n, r: pltpu.VMEM((n, r, LANES), F32)
    grid_spec = pltpu.PrefetchScalarGridSpec(
        num_scalar_prefetch=1,
        grid=(nb, ns),
        in_specs=in_specs,
        out_specs=pl.BlockSpec((None, SQ, 1024), lambda bb, j, pt: (bb, 0, 0)),
        scratch_shapes=[col(NPA, 2 * SQ), col(NPA, 2 * SQ), acc(NPA, 2 * SQ),
                        col(NPB, 4 * SQ), col(NPB, 4 * SQ), acc(NPB, 4 * SQ),
                        col(NPC, 2 * SQ), acc(NPC, 2 * SQ)],
    )
    return pl.pallas_call(
        functools.partial(_attn_s_kernel, g=g, n_pages=n_pages, lam_init=lam_init),
        grid_spec=grid_spec,
        out_shape=jax.ShapeDtypeStruct((nb, SQ, 1024), BF16),
        compiler_params=pltpu.CompilerParams(dimension_semantics=("arbitrary", "arbitrary"),
                                             vmem_limit_bytes=VMEM_LIMIT),
        name="attn_sample",
    )(page_table, *args)


def _proj_weights(w_in):
    splits = (384, 384, 384, 256, 4, 64, 256, 256, 256, 384, 384, 384)
    offs = np.concatenate([[0], np.cumsum(splits)])
    part = {n: w_in[:, :, offs[k]:offs[k + 1]] for k, n in enumerate(
        ("qa", "ka", "va", "qi", "wi", "ki", "qd", "kd", "vd", "qc", "kc", "vc"))}
    part["qa"] = part["qa"] * HEAD_DIM ** -0.5
    part["qc"] = part["qc"] * HEAD_DIM ** -0.5
    part["qi"] = part["qi"] * IDX_DIM ** -0.5
    part["qd"] = part["qd"] * HD_B ** -0.5
    part["wi"] = jnp.pad(part["wi"] * N_IDX_HEADS ** -0.5, ((0, 0), (0, 0), (0, LANES - N_IDX_HEADS)))
    part["ki2"] = jnp.concatenate([part["ki"], part["ki"]], axis=-1)
    wq = jnp.concatenate([part[n] for n in _QCOLS], axis=-1).astype(BF16)
    wk = jnp.swapaxes(jnp.concatenate([part[n] for n in _KROWS], axis=-1), 1, 2).astype(BF16)
    return wq, wk


def _row_tile(m):
    for tm in (512, 256, 128, 64, 32, 16, 8):
        if m % tm == 0:
            return tm
    raise ValueError(m)


def _feature_major(cache):
    nd = cache.ndim
    c = jnp.transpose(cache, (0, 1) + tuple(range(3, nd)) + (2,))
    return c.reshape(c.shape[0], c.shape[1], -1, c.shape[-1])


def kernel(x_prompt, x_sample, c_prompt, c_sample, cache_a_k, cache_a_v, cache_a_kidx, cache_b_k,
           cache_b_v, cache_c_k, cache_c_v, page_table, rel_bias, w_mod, b_mod, g_attn, g_ffn, w_in,
           diff_lambda, subln_g, w_out, w_ffn_in, w_ffn_out, g_final):
    b, t, d = x_prompt.shape
    nb, ts, _ = x_sample.shape
    depth = w_in.shape[0]
    page = cache_a_k.shape[2]
    n_pages = page_table.shape[1]
    assert page == BLK and t % (KT_PROMPT * BLK) == 0 and ts <= 8 and d == 1024
    g_att = 16 if n_pages % 16 == 0 else (4 if n_pages % 4 == 0 else 1)
    g_sel = 32 if n_pages % 32 == 0 else g_att
    topk_p = min(TOPK_MAX, t // 4)
    topk_s = min(TOPK_MAX, (n_pages * page + ts) // 4)

    wq, wk = _proj_weights(w_in)
    w_out_b = w_out.astype(BF16)
    w_ffn_in_b = w_ffn_in.astype(BF16)
    w_ffn_out_b = w_ffn_out.astype(BF16)
    g_attn3 = g_attn.reshape(depth, 1, d)
    g_ffn3 = g_ffn.reshape(depth, 1, d)
    g_final2 = g_final.reshape(1, d)
    g2 = jnp.concatenate([subln_g, subln_g], axis=-1).reshape(depth, 1, LANES)
    pools = [_feature_major(c) for c in (cache_a_k, cache_a_v, cache_b_k, cache_b_v, cache_c_k, cache_c_v)]
    pool_kidx = _feature_major(cache_a_kidx)

    mod = _modulation(jnp.concatenate([c_prompt, c_sample], axis=0), w_mod, b_mod)
    mod_p = mod[:, :, :b].reshape(depth, 6, b, 1, d)
    mod_s = jnp.repeat(mod[:, :, b:], ts, axis=2).reshape(depth, 6, 1, nb * ts, d)

    tiles = _bias_tiles(rel_bias)

    def stacked(heads, rows):
        return jnp.concatenate([tiles[h][:, :rows] for h in heads], axis=1)

    pairs_a = [(2 * p, 2 * p + 1) for p in range(NPA)]
    quads_b = [(H_A + 2 * u,) * 2 + (H_A + 2 * u + 1,) * 2 for u in range(NPB)]
    ta_p = jnp.stack([stacked(hs, BLK) for hs in pairs_a])
    tb_p = jnp.stack([stacked(hs, BLK) for hs in quads_b])
    ta_s = jnp.stack([stacked(hs, SQ) for hs in pairs_a])
    tb_s = jnp.stack([stacked(hs, SQ) for hs in quads_b])

    xp = x_prompt.reshape(b * t, d)
    xs = x_sample.reshape(nb * ts, d)
    tm_p = _row_tile(t)
    tm_s = nb * ts
    new_p = {k: [] for k in _KV}
    new_s = {k: [] for k in _KV}

    def pad_q(a):
        return jnp.pad(a.reshape(nb, ts, -1), ((0, 0), (0, SQ - ts), (0, 0)))

    def new_keys(a, w=None):
        a = a[0] if w is None else a[0, :w]
        a = jnp.transpose(a.reshape(a.shape[0], nb, ts), (1, 0, 2))
        return jnp.pad(a, ((0, 0), (0, 0), (0, BLK - ts))).astype(BF16)

    for l in range(depth):
        lam_init = 0.8 - 0.6 * math.exp(-0.3 * l)
        final = l == depth - 1
        pr = _in_proj(xp, g_attn3, mod_p, wq, wk, l, b, tm_p, t // tm_p, True)
        mix = _attn_prompt(pr, ta_p, tb_p, diff_lambda, g2, l, b, t, topk_p, lam_init)
        xp = _out_proj(mix.reshape(b * t, d), xp, mod_p, w_out_b, l, tm_p, t // tm_p)
        xp = _ffn(xp, g_ffn3, mod_p, w_ffn_in_b, w_ffn_out_b, g_final2, l, tm_p, t // tm_p, final)
        sr = _in_proj(xs, g_attn3, mod_s, wq, wk, l, 1, tm_s, None, False)
        qi_stack = pad_q(sr["qi"]).reshape(nb, SQ, N_IDX_HEADS, IDX_DIM).transpose(0, 2, 1, 3)
        qi_stack = qi_stack.reshape(nb, N_IDX_HEADS * SQ, IDX_DIM)
        addmask = _select_sample(page_table, qi_stack, pad_q(sr["wi"]), new_keys(sr["ki"]),
                                 pool_kidx, l, g_sel, topk_s)
        mix_s = _attn_sample(page_table,
                             [pad_q(sr["qa"]), pad_q(sr["qd"]), pad_q(sr["qc"])],
                             [new_keys(sr[n]) for n in ("ka", "va", "kd", "vd", "kc", "vc")],
                             addmask, ta_s, tb_s, diff_lambda, g2, pools, l, g_att, lam_init)
        mix_s = mix_s[:, :ts].reshape(nb * ts, d)
        xs = _out_proj(mix_s, xs, mod_s, w_out_b, l, tm_s, None)
        xs = _ffn(xs, g_ffn3, mod_s, w_ffn_in_b, w_ffn_out_b, g_final2, l, tm_s, None, final)
        for k in _KV:
            new_p[k].append(pr[k])
            new_s[k].append(sr[k])

    tails = {"ka": (H_A, HEAD_DIM), "va": (H_A, HEAD_DIM), "ki": (IDX_DIM,), "kd": (H_B, 2, HD_B),
             "vd": (H_B, 2 * HD_B), "kc": (H_C, HEAD_DIM), "vc": (H_C, HEAD_DIM)}

    def rows_out(arrs, lead, tail):
        a = jnp.stack(arrs)
        nbt, tt = a.shape[1], a.shape[3]
        a = a.reshape((depth, nbt) + tail + (tt,))
        nd = a.ndim
        a = jnp.transpose(a, (0, 1, nd - 1) + tuple(range(2, nd - 1)))
        return a.reshape((depth,) + lead + tail)

    outs_p = [rows_out(new_p[k], (b, t), tails[k]) for k in _KV]
    outs_s = [rows_out(new_s[k], (nb, ts), tails[k]) for k in _KV]
    return (xp.reshape(b, t, d), xs.reshape(nb, ts, d), *outs_p, *outs_s)
```

```python
import functools
import math

import numpy as np
import jax
import jax.numpy as jnp
from jax import lax
from jax.experimental import pallas as pl
from jax.experimental.pallas import tpu as pltpu

F32, BF16, I32 = jnp.float32, jnp.bfloat16, jnp.int32

HEAD_DIM = 64
H_A, H_B, H_C = 6, 4, 6
HD_B = HEAD_DIM // 2
N_IDX_HEADS = 4
IDX_DIM = 64
TOPK_MAX = 256
NUM_BUCKETS = 32
MAX_DISTANCE = 128
EPS = 1e-6
NEG = -1e30
LOG2E = 1.0 / math.log(2.0)
M_INIT = -3e38
INT_MIN = -2 ** 31
LANES = 128
BLK = 128
SQ = 16
KT_PROMPT = 4
VMEM_LIMIT = 56 * 1024 * 1024
NPA, NPB, NPC = H_A // 2, H_B // 2, H_C // 2


def _layout(groups):
    out, off = {}, 0
    for n, w in groups:
        out[n] = (off, w)
        off += w
    return out, off


_QCOLS, N_Q = _layout((("qa", 384), ("qi", 256), ("wi", 128), ("qd", 256), ("qc", 384)))
_KROWS, N_K = _layout((("ka", 384), ("va", 384), ("ki2", 128), ("kd", 256), ("vd", 256),
                       ("kc", 384), ("vc", 384)))
_KV = ("ka", "va", "ki", "kd", "vd", "kc", "vc")


def _krow(name):
    return (_KROWS["ki2"][0], IDX_DIM) if name == "ki" else _KROWS[name]


def _bucket_thresholds():
    max_exact = NUM_BUCKETS // 2
    n = np.arange(max_exact, 4 * MAX_DISTANCE, dtype=np.float32)
    large = max_exact + (np.log(n / max_exact) / math.log(MAX_DISTANCE / max_exact)
                         * (NUM_BUCKETS - max_exact)).astype(np.int32)
    large = np.minimum(large, NUM_BUCKETS - 1)
    return [int(n[np.argmax(large >= b)]) for b in range(max_exact + 1, NUM_BUCKETS)]


_THRS = _bucket_thresholds()


def _mm(a, b):
    return lax.dot_general(a.astype(BF16), b.astype(BF16), (((1,), (0,)), ((), ())),
                           preferred_element_type=F32)


def _mm_nt(a, b):
    return lax.dot_general(a.astype(BF16), b.astype(BF16), (((1,), (1,)), ((), ())),
                           preferred_element_type=F32)


def _split(a):
    hi = a.astype(BF16)
    lo = (a - hi.astype(F32)).astype(BF16)
    return hi, lo


def _mm3(a, b):
    ah, al = _split(a)
    bh, bl = _split(b)
    return _mm(ah, bh) + (_mm(ah, bl) + _mm(al, bh))


def _sigmoid(x):
    return 1.0 / (1.0 + jnp.exp(-x))


def _rmsnorm(x, g):
    return x * lax.rsqrt(jnp.mean(x * x, axis=-1, keepdims=True) + EPS) * g


def _lane():
    return lax.broadcasted_iota(I32, (1, LANES), 1)


def _stack_q(q, parts):
    grp = _lane() >> int(math.log2(LANES // parts))
    zero = jnp.zeros_like(q)
    return jnp.concatenate([jnp.where(grp == r, q, zero) for r in range(parts)], axis=0)


def _sel2(lo_val, hi_val):
    return jnp.where(_lane() < HEAD_DIM, lo_val, hi_val)


def _rowsum(x):
    return jnp.sum(x, axis=-1, keepdims=True)


def _tile_lanes(x, n):
    return x if n == 1 else jnp.concatenate([x] * n, axis=1)


def _softmax_update(m_ref, l_ref, acc_ref, idx, q_st, kk, vv, add, mxu_rowsum):
    s = _mm(q_st, kk) + add
    m_old = m_ref[idx]
    m_new = jnp.maximum(m_old, jnp.max(s, axis=-1, keepdims=True))
    p = jnp.exp2(s - _tile_lanes(m_new, s.shape[1] // LANES)).astype(BF16)
    alpha = jnp.exp2(m_old - m_new)
    if mxu_rowsum:
        both = _mm_nt(p, jnp.concatenate([vv.astype(BF16), jnp.ones(vv.shape, BF16)], axis=0))
        pv, psum = both[:, :LANES], both[:, LANES:]
    else:
        pv, psum = _mm_nt(p, vv), _rowsum(p.astype(F32))
    l_ref[idx] = alpha * l_ref[idx] + psum
    acc_ref[idx] = alpha * acc_ref[idx] + pv
    m_ref[idx] = m_new


def _softmax_init(m_ref, l_ref, acc_ref):
    m_ref[...] = jnp.full(m_ref.shape, M_INIT, F32)
    l_ref[...] = jnp.zeros(l_ref.shape, F32)
    acc_ref[...] = jnp.zeros(acc_ref.shape, F32)


def _pair_out(acc, l, tq):
    o = acc / l
    return _sel2(o[:tq], o[tq:])


def _diff_out(acc, l, tq, lam, g2, out_scale):
    o = acc / l
    o = _sel2(o[:tq], o[2 * tq:3 * tq]) - lam * _sel2(o[tq:2 * tq], o[3 * tq:])
    o2 = o * o
    lo = _lane() < HEAD_DIM
    ms_lo = _rowsum(jnp.where(lo, o2, 0.0)) * (1.0 / HEAD_DIM)
    ms_hi = _rowsum(jnp.where(lo, 0.0, o2)) * (1.0 / HEAD_DIM)
    r = _sel2(lax.rsqrt(ms_lo + EPS), lax.rsqrt(ms_hi + EPS))
    return o * r * g2 * out_scale


def _later_matrix():
    j = lax.broadcasted_iota(I32, (BLK, 2 * BLK), 0)
    s = lax.broadcasted_iota(I32, (BLK, 2 * BLK), 1)
    return ((j > s) | (s >= BLK)).astype(BF16)


def _stick_update(c_ref, acc_ref, q_sts, kks, vvs, strict, later_mat):
    n = kks[0].shape[1] // BLK
    rows = q_sts[0].shape[0]
    lbs, lks = [], []
    for q_st, kk in zip(q_sts, kks):
        z = _mm(q_st, kk)
        lb = jnp.minimum(z, 0.0) - jnp.log(1.0 + jnp.exp(-jnp.abs(z)))
        lk = lb - z
        if strict is not None:
            lk = jnp.where(strict, lk, 0.0)
        lk = lk.astype(BF16)
        lbs.append(lb)
        lks += [lk[:, r * BLK:(r + 1) * BLK] for r in range(n)]
    both = _mm(jnp.concatenate(lks, axis=0), later_mat)
    for idx, (lb, vv) in enumerate(zip(lbs, vvs)):
        c = c_ref[idx]
        laters = [None] * n
        for r in reversed(range(n)):
            part = both[(idx * n + r) * rows:(idx * n + r + 1) * rows]
            laters[r] = part[:, :BLK] + c
            c = c + part[:, BLK:]
        later = laters[0] if n == 1 else jnp.concatenate(laters, axis=1)
        a = jnp.exp(lb + later)
        if strict is not None:
            a = jnp.where(strict, a, 0.0)
        acc_ref[idx] = acc_ref[idx] + _mm_nt(a, vv)
        c_ref[idx] = c


def _sortable(score):
    i = lax.bitcast_convert_type(score + 0.0, I32)
    return i ^ ((i >> 31) & 0x7FFFFFFF)


def _before_matrix():
    jj = lax.broadcasted_iota(I32, (BLK, BLK), 0)
    ss = lax.broadcasted_iota(I32, (BLK, BLK), 1)
    return (jj < ss).astype(BF16)


def _kth_largest(count_ge, tq, topk):
    kf = float(topk)
    zero = jnp.zeros((tq, 1), I32)
    v = jnp.where(count_ge(zero) >= kf, zero, jnp.full((tq, 1), INT_MIN, I32))

    def bit_body(bi, v):
        cand = v | lax.shift_left(jnp.int32(1), 30 - bi)
        return jnp.where(count_ge(cand) >= kf, cand, v)

    return lax.fori_loop(0, 31, bit_body, v)


def _topk_select_blocks(keys_ref, mask_ref, ngrp, grp, topk, tq, valid_fn):
    def count(pred):
        def body(gi, acc):
            for c in range(grp):
                acc = acc + pred(keys_ref[gi * grp + c]).astype(F32)
            return acc
        return _rowsum(lax.fori_loop(0, ngrp, body, jnp.zeros((tq, LANES), F32)))

    v = _kth_largest(lambda cand: count(lambda key: key >= cand), tq, topk)
    need = float(topk) - count(lambda key: key > v)
    before = _before_matrix()

    def sel_body(gi, run):
        keys = [keys_ref[gi * grp + c] for c in range(grp)]
        eqs = [(key == v).astype(F32) for key in keys]
        ranks = _mm(jnp.concatenate(eqs, axis=0), before)
        for c, (key, eqf) in enumerate(zip(keys, eqs)):
            kb = gi * grp + c
            rank = ranks[c * tq:(c + 1) * tq] + run
            sel = (key > v) | ((key == v) & (rank < need))
            mask_ref[kb] = jnp.where(sel & valid_fn(kb), 0.0, NEG)
            run = run + _rowsum(eqf)
        return run

    lax.fori_loop(0, ngrp, sel_body, jnp.zeros((tq, 1), F32))


def _topk_select_wide(keys_ref, valid, topk, tq):
    def count(pred):
        return _rowsum(pred(keys_ref[...]).astype(F32))

    v = _kth_largest(lambda cand: count(lambda key: key >= cand), tq, topk)
    need = float(topk) - count(lambda key: key > v)
    before = _before_matrix()
    run = jnp.zeros((tq, 1), F32)
    masks = []
    for kb in range(keys_ref.shape[1] // BLK):
        sl = slice(kb * BLK, (kb + 1) * BLK)
        key = keys_ref[:, sl]
        eqf = (key == v).astype(F32)
        rank = _mm(eqf, before) + run
        sel = (key > v) | ((key == v) & (rank < need))
        if valid[kb] is not None:
            sel = sel & valid[kb]
        masks.append(jnp.where(sel, 0.0, NEG))
        run = run + _rowsum(eqf)
    return jnp.concatenate(masks, axis=1)


def _stacked_index_scores(q_stack, kk, wi, tq):
    s = jnp.maximum(_mm(q_stack, kk), 0.0)
    score = None
    for n in range(N_IDX_HEADS):
        term = s[n * tq:(n + 1) * tq] * wi[:, n:n + 1]
        score = term if score is None else score + term
    return score


def _mod_kernel(c_ref, w_ref, b_ref, o_ref):
    c = c_ref[...]
    o_ref[...] = _mm3(c * _sigmoid(c), w_ref[...]) + b_ref[...]


def _modulation(c_all, w_mod, b_mod):
    depth, d, _ = w_mod.shape
    n = c_all.shape[0]
    return pl.pallas_call(
        _mod_kernel,
        grid=(depth, 6),
        in_specs=[pl.BlockSpec((n, d), lambda l, c: (0, 0)),
                  pl.BlockSpec((None, d, d), lambda l, c: (l, 0, c)),
                  pl.BlockSpec((None, None, 1, d), lambda l, c: (l, c, 0, 0))],
        out_specs=pl.BlockSpec((None, None, n, d), lambda l, c: (l, c, 0, 0)),
        out_shape=jax.ShapeDtypeStruct((depth, 6, n, d), F32),
        compiler_params=pltpu.CompilerParams(dimension_semantics=("arbitrary", "arbitrary"),
                                             vmem_limit_bytes=VMEM_LIMIT),
        name="modulation",
    )(c_all, w_mod, b_mod.reshape(depth, 6, 1, d))


def _bias_kernel(rb_ref, o_ref):
    h = pl.program_id(0)
    t = lax.broadcasted_iota(I32, (BLK, BLK), 0)
    s = lax.broadcasted_iota(I32, (BLK, BLK), 1)
    for ty, c in ((0, 0), (1, BLK)):
        rel = c + t - s
        n = jnp.maximum(rel, 0)
        large = jnp.full((BLK, BLK), NUM_BUCKETS // 2, I32)
        for thr in _THRS:
            large = large + (n >= thr).astype(I32)
        bucket = jnp.where(n < NUM_BUCKETS // 2, n, large)
        val = jnp.zeros((BLK, BLK), F32)
        for b in range(NUM_BUCKETS):
            val = jnp.where(bucket == b, rb_ref[b, h], val)
        val = val * LOG2E
        if ty == 0:
            val = jnp.where(rel >= 0, val, NEG)
        o_ref[ty] = val
    o_ref[2] = jnp.full((BLK, BLK), rb_ref[NUM_BUCKETS - 1, h] * LOG2E, F32)
    o_ref[3] = jnp.full((BLK, BLK), NEG, F32)


def _bias_tiles(rel_bias):
    nh = rel_bias.shape[1]
    return pl.pallas_call(
        _bias_kernel,
        grid=(nh,),
        in_specs=[pl.BlockSpec(memory_space=pltpu.SMEM)],
        out_specs=pl.BlockSpec((None, 4, BLK, BLK), lambda h: (h, 0, 0, 0)),
        out_shape=jax.ShapeDtypeStruct((nh, 4, BLK, BLK), F32),
        name="bias_tiles",
    )(rel_bias)


_Q_OUTS = (("qa", BF16), ("qi", BF16), ("wi", F32), ("qd", BF16), ("qc", BF16))
_KB_OUTS = ("ka", "va", "ki2", "kd", "vd", "kc", "vc")


def _in_kernel(x_ref, g_ref, sh_ref, sc_ref, wq_ref, wk_ref, *outs, tiled):
    x = x_ref[...]
    h = (_rmsnorm(x, g_ref[...]) * (1.0 + sc_ref[...]) + sh_ref[...]).astype(BF16)
    yq = _mm(h, wq_ref[...])
    yk = _mm_nt(wk_ref[...], h)
    outs = list(outs)
    for name, dt in _Q_OUTS:
        off, w = _QCOLS[name]
        outs.pop(0)[...] = yq[:, off:off + w].astype(dt)
    for name in _KV:
        off, w = _krow(name)
        outs.pop(0)[...] = yk[off:off + w, :]
    if tiled:
        for name in _KB_OUTS:
            off, w = _KROWS[name]
            o = outs.pop(0)
            for c in range(o.shape[0]):
                o[c] = yk[off:off + w, c * BLK:(c + 1) * BLK].astype(BF16)


def _mod_spec(tm, d, rows_per_mod, l, c):
    if rows_per_mod is None:
        return pl.BlockSpec((None, None, None, tm, d), lambda i: (l, c, 0, i, 0))
    return pl.BlockSpec((None, None, None, 1, d), lambda i: (l, c, i // rows_per_mod, 0, 0))


def _in_proj(x, g, mod, wq, wk, l, nbatch, tm, tiles_per_mod, tiled):
    m, d = x.shape
    t = m // nbatch
    tpb = t // tm
    out_shapes, out_specs = [], []
    for name, dt in _Q_OUTS:
        w = _QCOLS[name][1]
        out_shapes.append(jax.ShapeDtypeStruct((m, w), dt))
        out_specs.append(pl.BlockSpec((tm, w), lambda i: (i, 0)))
    for name in _KV:
        w = _krow(name)[1]
        out_shapes.append(jax.ShapeDtypeStruct((nbatch, w, t), F32))
        out_specs.append(pl.BlockSpec((None, w, tm), lambda i: (i // tpb, 0, i % tpb)))
    if tiled:
        for name in _KB_OUTS:
            w = _KROWS[name][1]
            out_shapes.append(jax.ShapeDtypeStruct((nbatch, t // BLK, w, BLK), BF16))
            out_specs.append(pl.BlockSpec((None, tm // BLK, w, BLK), lambda i: (i // tpb, i % tpb, 0, 0)))
    outs = pl.pallas_call(
        functools.partial(_in_kernel, tiled=tiled),
        grid=(m // tm,),
        in_specs=[pl.BlockSpec((tm, d), lambda i: (i, 0)),
                  pl.BlockSpec((None, 1, d), lambda i: (l, 0, 0)),
                  _mod_spec(tm, d, tiles_per_mod, l, 0),
                  _mod_spec(tm, d, tiles_per_mod, l, 1),
                  pl.BlockSpec((None, d, N_Q), lambda i: (l, 0, 0)),
                  pl.BlockSpec((None, N_K, d), lambda i: (l, 0, 0))],
        out_specs=out_specs,
        out_shape=out_shapes,
        compiler_params=pltpu.CompilerParams(dimension_semantics=("arbitrary",),
                                             vmem_limit_bytes=VMEM_LIMIT),
        name="in_proj",
    )(x, g, mod, mod, wq, wk)
    names = [n for n, _ in _Q_OUTS] + list(_KV) + ([n + "_t" for n in _KB_OUTS] if tiled else [])
    return dict(zip(names, outs))


def _out_kernel(mix_ref, x_ref, gt_ref, w_ref, o_ref):
    o_ref[...] = x_ref[...] + gt_ref[...] * _mm(mix_ref[...], w_ref[...])


def _out_proj(mix, x, mod, w_out, l, tm, tiles_per_mod):
    m, d = x.shape
    return pl.pallas_call(
        _out_kernel,
        grid=(m // tm,),
        in_specs=[pl.BlockSpec((tm, d), lambda i: (i, 0)),
                  pl.BlockSpec((tm, d), lambda i: (i, 0)),
                  _mod_spec(tm, d, tiles_per_mod, l, 2),
                  pl.BlockSpec((None, d, d), lambda i: (l, 0, 0))],
        out_specs=pl.BlockSpec((tm, d), lambda i: (i, 0)),
        out_shape=jax.ShapeDtypeStruct((m, d), F32),
        compiler_params=pltpu.CompilerParams(dimension_semantics=("arbitrary",),
                                             vmem_limit_bytes=VMEM_LIMIT),
        name="out_proj",
    )(mix, x, mod, w_out)


def _ffn_kernel(x_ref, g_ref, sh_ref, sc_ref, gt_ref, wg_ref, wu_ref, wo_ref, gf_ref, o_ref,
                h_ref, acc_ref, *, final):
    j = pl.program_id(1)

    @pl.when(j == 0)
    def _():
        h = _rmsnorm(x_ref[...], g_ref[...]) * (1.0 + sc_ref[...]) + sh_ref[...]
        h_ref[...] = h.astype(BF16)
        acc_ref[...] = jnp.zeros_like(acc_ref)

    h = h_ref[...]
    gate = _mm(h, wg_ref[...])
    up = _mm(h, wu_ref[...])
    acc_ref[...] += _mm(gate * _sigmoid(gate) * up, wo_ref[...])

    @pl.when(j == pl.num_programs(1) - 1)
    def _():
        y = x_ref[...] + gt_ref[...] * acc_ref[...]
        if final:
            y = _rmsnorm(y, gf_ref[...])
        o_ref[...] = y


def _ffn(x, g, mod, w_in, w_out, g_final, l, tm, tiles_per_mod, final):
    m, d = x.shape
    dff = w_out.shape[1]
    nf = 2 if dff % (2 * LANES) == 0 else 1
    tf = dff // nf

    def ms(c):
        spec = _mod_spec(tm, d, tiles_per_mod, l, c)
        return pl.BlockSpec(spec.block_shape, lambda i, j, f=spec.index_map: f(i))

    return pl.pallas_call(
        functools.partial(_ffn_kernel, final=final),
        grid=(m // tm, nf),
        in_specs=[pl.BlockSpec((tm, d), lambda i, j: (i, 0)),
                  pl.BlockSpec((None, 1, d), lambda i, j: (l, 0, 0)),
                  ms(3), ms(4), ms(5),
                  pl.BlockSpec((None, d, tf), lambda i, j: (l, 0, j)),
                  pl.BlockSpec((None, d, tf), lambda i, j: (l, 0, nf + j)),
                  pl.BlockSpec((None, tf, d), lambda i, j: (l, j, 0)),
                  pl.BlockSpec((1, d), lambda i, j: (0, 0))],
        out_specs=pl.BlockSpec((tm, d), lambda i, j: (i, 0)),
        out_shape=jax.ShapeDtypeStruct((m, d), F32),
        scratch_shapes=[pltpu.VMEM((tm, d), BF16), pltpu.VMEM((tm, d), F32)],
        compiler_params=pltpu.CompilerParams(dimension_semantics=("arbitrary", "arbitrary"),
                                             vmem_limit_bytes=VMEM_LIMIT),
        name="ffn",
    )(x, g, mod, mod, mod, w_in, w_in, w_out, g_final)


def _lambda(dl_ref, lam_init):
    dl = dl_ref[...]
    a = jnp.sum(dl[0:1] * dl[1:2], axis=(0, 1), keepdims=True)
    b = jnp.sum(dl[2:3] * dl[3:4], axis=(0, 1), keepdims=True)
    return jnp.exp(a) - jnp.exp(b) + lam_init


def _blk(p):
    return slice(p * LANES, (p + 1) * LANES)


def _attn_p_kernel(qa_ref, ka_ref, va_ref, qi_ref, wi_ref, ki2_ref, qd_ref, kd_ref, vd_ref,
                   qc_ref, kc_ref, vc_ref, ta_ref, tb_ref, dl_ref, g2_ref, o_ref,
                   keys_ref, mask_ref, ma, la, acca, mb, lb, accb, cc, accc, *, topk, lam_init, kt):
    i = pl.program_id(1)
    tq = BLK
    nblk = i + 1
    t_loc = lax.broadcasted_iota(I32, (tq, BLK), 0)
    s_loc = lax.broadcasted_iota(I32, (tq, BLK), 1)

    def causal(kb):
        return kb * BLK + s_loc <= i * BLK + t_loc

    qi2 = [_stack_q(qi_ref[:, _blk(b)], 2) for b in range(N_IDX_HEADS // 2)]
    wi = wi_ref[...]
    wib = [jnp.broadcast_to(wi[:, n:n + 1], (tq, kt * BLK)) for n in range(N_IDX_HEADS)]
    ngrp = i // kt + 1

    def score_body(gi, carry):
        kk = jnp.concatenate([ki2_ref[gi * kt + c] for c in range(kt)], axis=1)
        score = None
        for b in range(N_IDX_HEADS // 2):
            s = jnp.maximum(_mm(qi2[b], kk), 0.0)
            term = s[:tq] * wib[2 * b] + s[tq:] * wib[2 * b + 1]
            score = term if score is None else score + term
        key = _sortable(score)
        for c in range(kt):
            kb = gi * kt + c
            keys_ref[kb] = jnp.where(causal(kb), key[:, c * BLK:(c + 1) * BLK], INT_MIN)
        return carry

    lax.fori_loop(0, ngrp, score_body, 0)
    _topk_select_blocks(keys_ref, mask_ref, ngrp, kt, topk, tq, causal)

    qa2 = [_stack_q(qa_ref[:, _blk(p)], 2) for p in range(NPA)]
    qd4 = [_stack_q(qd_ref[:, _blk(u)], 4) for u in range(NPB)]
    qc2 = [_stack_q(qc_ref[:, _blk(p)], 2) for p in range(NPC)]
    later_mat = _later_matrix()
    _softmax_init(ma, la, acca)
    _softmax_init(mb, lb, accb)
    cc[...] = jnp.zeros(cc.shape, F32)
    accc[...] = jnp.zeros(accc.shape, F32)
    t2 = lax.broadcasted_iota(I32, (2 * tq, BLK), 0) & (tq - 1)
    s2 = lax.broadcasted_iota(I32, (2 * tq, BLK), 1)

    def body(r, carry, diagonal, ntile=kt):
        first = (i // kt - r) * kt
        kbs = [first + c for c in range(ntile)]
        tys = [jnp.where(kb > i, 3, jnp.minimum(i - kb, 2)) for kb in kbs]

        def cat(fn):
            parts = [fn(c) for c in range(ntile)]
            return parts[0] if ntile == 1 else jnp.concatenate(parts, axis=1)

        def keys(ref, rows):
            return cat(lambda c: ref[kbs[c], rows, :])

        def twice(x):
            return jnp.concatenate([x, x], axis=0)

        am2 = cat(lambda c: twice(mask_ref[jnp.minimum(kbs[c], i)]))
        for p in range(NPA):
            _softmax_update(ma, la, acca, p, qa2[p], keys(ka_ref, _blk(p)), keys(va_ref, _blk(p)),
                            cat(lambda c: ta_ref[p, tys[c]]) + am2, True)
        for u in range(NPB):
            _softmax_update(mb, lb, accb, u, qd4[u], keys(kd_ref, _blk(u)), keys(vd_ref, _blk(u)),
                            cat(lambda c: tb_ref[u, tys[c]]), True)
        strict = cat(lambda c: kbs[c] * BLK + s2 < i * BLK + t2) if diagonal else None
        _stick_update(cc, accc, qc2, [keys(kc_ref, _blk(p)) for p in range(NPC)],
                      [keys(vc_ref, _blk(p)) for p in range(NPC)], strict, later_mat)
        return carry

    for c in range(kt):
        @pl.when(i % kt == c)
        def _(c=c):
            body(0, 0, diagonal=True, ntile=c + 1)

    lax.fori_loop(1, i // kt + 1, functools.partial(body, diagonal=False), 0)

    for p in range(NPA):
        o_ref[:, _blk(p)] = _pair_out(acca[p], la[p], tq).astype(o_ref.dtype)
    lam = _lambda(dl_ref, lam_init)
    g2 = g2_ref[...]
    for u in range(NPB):
        o_ref[:, _blk(NPA + u)] = _diff_out(accb[u], lb[u], tq, lam, g2, 1.0 - lam_init).astype(o_ref.dtype)
    for p in range(NPC):
        acc = accc[p]
        o_ref[:, _blk(NPA + NPB + p)] = _sel2(acc[:tq], acc[tq:]).astype(o_ref.dtype)


def _attn_prompt(pr, ta, tb, dl, g2, l, b, t, topk, lam_init):
    nq = t // BLK

    def qspec(w):
        return pl.BlockSpec((None, BLK, w), lambda bb, i: (bb, i, 0))

    def kspec(w):
        return pl.BlockSpec((None, nq, w, BLK), lambda bb, i: (bb, 0, 0, 0))

    def full(a):
        return pl.BlockSpec(a.shape, lambda bb, i, n=a.ndim: (0,) * n)

    r3 = lambda a: a.reshape(b, t, a.shape[-1])
    args = [r3(pr["qa"]), pr["ka_t"], pr["va_t"], r3(pr["qi"]), r3(pr["wi"]), pr["ki2_t"],
            r3(pr["qd"]), pr["kd_t"], pr["vd_t"], r3(pr["qc"]), pr["kc_t"], pr["vc_t"]]
    specs = [qspec(384), kspec(384), kspec(384), qspec(256), qspec(128), kspec(128),
             qspec(256), kspec(256), kspec(256), qspec(384), kspec(384), kspec(384)]
    col = lambda n, r: pltpu.VMEM((n, r, LANES), F32)
    acc = lambda n, r: pltpu.VMEM((n, r, LANES), F32)
    return pl.pallas_call(
        functools.partial(_attn_p_kernel, topk=topk, lam_init=lam_init, kt=KT_PROMPT),
        grid=(b, nq),
        in_specs=specs + [full(ta), full(tb),
                          pl.BlockSpec((None, 4, HD_B), lambda bb, i: (l, 0, 0)),
                          pl.BlockSpec((None, 1, LANES), lambda bb, i: (l, 0, 0))],
        out_specs=pl.BlockSpec((None, BLK, 1024), lambda bb, i: (bb, i, 0)),
        out_shape=jax.ShapeDtypeStruct((b, t, 1024), BF16),
        scratch_shapes=[pltpu.VMEM((nq, BLK, BLK), I32), pltpu.VMEM((nq, BLK, BLK), F32),
                        col(NPA, 2 * BLK), col(NPA, 2 * BLK), acc(NPA, 2 * BLK),
                        col(NPB, 4 * BLK), col(NPB, 4 * BLK), acc(NPB, 4 * BLK),
                        col(NPC, 2 * BLK), acc(NPC, 2 * BLK)],
        compiler_params=pltpu.CompilerParams(dimension_semantics=("arbitrary", "arbitrary"),
                                             vmem_limit_bytes=VMEM_LIMIT),
        name="attn_prompt",
    )(*args, ta, tb, dl, g2)


def _sel_s_kernel(pt_ref, qi_ref, wi_ref, kn_ref, *rest, g, n_pages, topk):
    page_refs = rest[:g]
    o_ref = rest[g]
    keys_ref, wide_ref = rest[g + 1], rest[g + 2]
    j = pl.program_id(1)
    ns = n_pages // g
    q = qi_ref[...]
    wi = wi_ref[...]
    kk = jnp.concatenate([page_refs[r][...].astype(BF16) for r in range(g)], axis=1)
    keys_ref[j] = _sortable(_stacked_index_scores(q, kk, wi, SQ))

    @pl.when(j == ns - 1)
    def _():
        t_loc = lax.broadcasted_iota(I32, (SQ, BLK), 0)
        s_loc = lax.broadcasted_iota(I32, (SQ, BLK), 1)
        new_ok = s_loc <= t_loc
        for jj in range(ns):
            wide_ref[:, jj * g * BLK:(jj + 1) * g * BLK] = keys_ref[jj]
        score = _stacked_index_scores(q, kn_ref[...], wi, SQ)
        wide_ref[:, n_pages * BLK:] = jnp.where(new_ok, _sortable(score), INT_MIN)
        o_ref[...] = _topk_select_wide(wide_ref, [None] * n_pages + [new_ok], topk, SQ)


def _select_sample(page_table, qi_stack, wi, ki_new, pool_kidx, l, g, topk):
    nb, n_pages = page_table.shape
    ns = n_pages // g
    width = (n_pages + 1) * BLK

    def page_spec(r):
        return pl.BlockSpec((None, None, IDX_DIM, BLK),
                            lambda bb, j, pt, r=r: (l, pt[bb, j * g + r], 0, 0))

    grid_spec = pltpu.PrefetchScalarGridSpec(
        num_scalar_prefetch=1,
        grid=(nb, ns),
        in_specs=[pl.BlockSpec((None, N_IDX_HEADS * SQ, IDX_DIM), lambda bb, j, pt: (bb, 0, 0)),
                  pl.BlockSpec((None, SQ, LANES), lambda bb, j, pt: (bb, 0, 0)),
                  pl.BlockSpec((None, IDX_DIM, BLK), lambda bb, j, pt: (bb, 0, 0))]
                 + [page_spec(r) for r in range(g)],
        out_specs=pl.BlockSpec((None, SQ, width), lambda bb, j, pt: (bb, 0, 0)),
        scratch_shapes=[pltpu.VMEM((ns, SQ, g * BLK), I32), pltpu.VMEM((SQ, width), I32)],
    )
    return pl.pallas_call(
        functools.partial(_sel_s_kernel, g=g, n_pages=n_pages, topk=topk),
        grid_spec=grid_spec,
        out_shape=jax.ShapeDtypeStruct((nb, SQ, width), F32),
        compiler_params=pltpu.CompilerParams(dimension_semantics=("arbitrary", "arbitrary"),
                                             vmem_limit_bytes=VMEM_LIMIT),
        name="select_sample",
    )(page_table, qi_stack, wi, ki_new, *([pool_kidx] * g))


def _attn_s_kernel(pt_ref, qa_ref, qd_ref, qc_ref, kan_ref, van_ref, kdn_ref, vdn_ref, kcn_ref,
                   vcn_ref, mnew_ref, mask_ref, ta_ref, tb_ref, dl_ref, g2_ref, *rest,
                   g, n_pages, lam_init):
    pools = [rest[k * g:(k + 1) * g] for k in range(6)]
    o_ref = rest[6 * g]
    ma, la, acca, mb, lb, accb, cc, accc = rest[6 * g + 1:]
    j = pl.program_id(1)
    ns = pl.num_programs(1)
    tq = SQ
    later_mat = _later_matrix()
    qa2 = [_stack_q(qa_ref[:, _blk(p)], 2) for p in range(NPA)]
    qd4 = [_stack_q(qd_ref[:, _blk(u)], 4) for u in range(NPB)]
    qc2 = [_stack_q(qc_ref[:, _blk(p)], 2) for p in range(NPC)]

    def twice(x):
        return jnp.concatenate([x, x], axis=0)

    @pl.when(j == 0)
    def _():
        _softmax_init(ma, la, acca)
        _softmax_init(mb, lb, accb)
        cc[...] = jnp.zeros(cc.shape, F32)
        accc[...] = jnp.zeros(accc.shape, F32)
        am2 = twice(mnew_ref[...])
        for p in range(NPA):
            _softmax_update(ma, la, acca, p, qa2[p], kan_ref[_blk(p), :], van_ref[_blk(p), :],
                            ta_ref[p, 0] + am2, False)
        for u in range(NPB):
            _softmax_update(mb, lb, accb, u, qd4[u], kdn_ref[_blk(u), :], vdn_ref[_blk(u), :],
                            tb_ref[u, 0], False)
        t2 = lax.broadcasted_iota(I32, (2 * tq, BLK), 0) & (tq - 1)
        s2 = lax.broadcasted_iota(I32, (2 * tq, BLK), 1)
        _stick_update(cc, accc, qc2, [kcn_ref[_blk(p), :] for p in range(NPC)],
                      [vcn_ref[_blk(p), :] for p in range(NPC)], s2 < t2, later_mat)

    def pages(kind, rows):
        return jnp.concatenate([pools[kind][r][rows, :].astype(BF16) for r in range(g)], axis=1)

    ty_last = jnp.where(j == 0, 1, 2)

    def bias(ref, idx):
        return jnp.concatenate([ref[idx, 2]] * (g - 1) + [ref[idx, ty_last]], axis=1)

    am2 = twice(mask_ref[...])
    for p in range(NPA):
        _softmax_update(ma, la, acca, p, qa2[p], pages(0, _blk(p)), pages(1, _blk(p)),
                        bias(ta_ref, p) + am2, False)
    for u in range(NPB):
        _softmax_update(mb, lb, accb, u, qd4[u], pages(2, _blk(u)), pages(3, _blk(u)),
                        bias(tb_ref, u), False)
    _stick_update(cc, accc, qc2, [pages(4, _blk(p)) for p in range(NPC)],
                  [pages(5, _blk(p)) for p in range(NPC)], None, later_mat)

    @pl.when(j == ns - 1)
    def _():
        lam = _lambda(dl_ref, lam_init)
        g2 = g2_ref[...]
        for p in range(NPA):
            o_ref[:, _blk(p)] = _pair_out(acca[p], la[p], tq).astype(o_ref.dtype)
        for u in range(NPB):
            o_ref[:, _blk(NPA + u)] = _diff_out(accb[u], lb[u], tq, lam, g2,
                                               1.0 - lam_init).astype(o_ref.dtype)
        for p in range(NPC):
            acc = accc[p]
            o_ref[:, _blk(NPA + NPB + p)] = _sel2(acc[:tq], acc[tq:]).astype(o_ref.dtype)


def _attn_sample(page_table, q_arrays, new_arrays, addmask, ta, tb, dl, g2, pools, l, g, lam_init):
    nb, n_pages = page_table.shape
    ns = n_pages // g

    def seq_spec(a):
        return pl.BlockSpec((None,) + a.shape[1:], lambda bb, j, pt: (bb, 0, 0))

    def full(a):
        return pl.BlockSpec(a.shape, lambda bb, j, pt, n=a.ndim: (0,) * n)

    def page_spec(w, r):
        return pl.BlockSpec((None, None, w, BLK),
                            lambda bb, j, pt, r=r: (l, pt[bb, n_pages - (j + 1) * g + r], 0, 0))

    in_specs = [seq_spec(a) for a in q_arrays] + [seq_spec(a) for a in new_arrays]
    in_specs += [pl.BlockSpec((None, SQ, BLK), lambda bb, j, pt: (bb, 0, n_pages)),
                 pl.BlockSpec((None, SQ, g * BLK), lambda bb, j, pt: (bb, 0, ns - 1 - j)),
                 full(ta), full(tb),
                 pl.BlockSpec((None, 4, HD_B), lambda bb, j, pt: (l, 0, 0)),
                 pl.BlockSpec((None, 1, LANES), lambda bb, j, pt: (l, 0, 0))]
    args = list(q_arrays) + list(new_arrays) + [addmask, addmask, ta, tb, dl, g2]
    for pool in pools:
        w = pool.shape[-2]
        in_specs += [page_spec(w, r) for r in range(g)]
        args += [pool] * g
    col = lambda n, r: pltpu.VMEM((n, r, LANES), F32)
    acc = lambda n, r: pltpu.VMEM((n, r, LANES), F32)
    grid_spec = pltpu.PrefetchScalarGridSpec(
        num_scalar_prefetch=1,
        grid=(nb, ns),
        in_specs=in_specs,
        out_specs=pl.BlockSpec((None, SQ, 1024), lambda bb, j, pt: (bb, 0, 0)),
        scratch_shapes=[col(NPA, 2 * SQ), col(NPA, 2 * SQ), acc(NPA, 2 * SQ),
                        col(NPB, 4 * SQ), col(NPB, 4 * SQ), acc(NPB, 4 * SQ),
                        col(NPC, 2 * SQ), acc(NPC, 2 * SQ)],
    )
    return pl.pallas_call(
        functools.partial(_attn_s_kernel, g=g, n_pages=n_pages, lam_init=lam_init),
        grid_spec=grid_spec,
        out_shape=jax.ShapeDtypeStruct((nb, SQ, 1024), BF16),
        compiler_params=pltpu.CompilerParams(dimension_semantics=("arbitrary", "arbitrary"),
                                             vmem_limit_bytes=VMEM_LIMIT),
        name="attn_sample",
    )(page_table, *args)


def _proj_weights(w_in):
    splits = (384, 384, 384, 256, 4, 64, 256, 256, 256, 384, 384, 384)
    offs = np.concatenate([[0], np.cumsum(splits)])
    part = {n: w_in[:, :, offs[k]:offs[k + 1]] for k, n in enumerate(
        ("qa", "ka", "va", "qi", "wi", "ki", "qd", "kd", "vd", "qc", "kc", "vc"))}
    part["qa"] = part["qa"] * (HEAD_DIM ** -0.5 * LOG2E)
    part["qc"] = part["qc"] * HEAD_DIM ** -0.5
    part["qi"] = part["qi"] * IDX_DIM ** -0.5
    part["qd"] = part["qd"] * (HD_B ** -0.5 * LOG2E)
    part["wi"] = jnp.pad(part["wi"] * N_IDX_HEADS ** -0.5, ((0, 0), (0, 0), (0, LANES - N_IDX_HEADS)))
    part["ki2"] = jnp.concatenate([part["ki"], part["ki"]], axis=-1)
    wq = jnp.concatenate([part[n] for n in _QCOLS], axis=-1).astype(BF16)
    wk = jnp.swapaxes(jnp.concatenate([part[n] for n in _KROWS], axis=-1), 1, 2).astype(BF16)
    return wq, wk


def _row_tile(m):
    for tm in (512, 256, 128, 64, 32, 16, 8):
        if m % tm == 0:
            return tm
    raise ValueError(m)


def _feature_major(cache):
    nd = cache.ndim
    c = jnp.transpose(cache, (0, 1) + tuple(range(3, nd)) + (2,))
    return c.reshape(c.shape[0], c.shape[1], -1, c.shape[-1])


def kernel(x_prompt, x_sample, c_prompt, c_sample, cache_a_k, cache_a_v, cache_a_kidx, cache_b_k,
           cache_b_v, cache_c_k, cache_c_v, page_table, rel_bias, w_mod, b_mod, g_attn, g_ffn, w_in,
           diff_lambda, subln_g, w_out, w_ffn_in, w_ffn_out, g_final):
    b, t, d = x_prompt.shape
    nb, ts, _ = x_sample.shape
    depth = w_in.shape[0]
    page = cache_a_k.shape[2]
    n_pages = page_table.shape[1]
    assert page == BLK and t % (KT_PROMPT * BLK) == 0 and ts <= 8 and d == 1024
    g_att = 16 if n_pages % 16 == 0 else (4 if n_pages % 4 == 0 else 1)
    g_sel = 32 if n_pages % 32 == 0 else g_att
    topk_p = min(TOPK_MAX, t // 4)
    topk_s = min(TOPK_MAX, (n_pages * page + ts) // 4)

    wq, wk = _proj_weights(w_in)
    w_out_b = w_out.astype(BF16)
    w_ffn_in_b = w_ffn_in.astype(BF16)
    w_ffn_out_b = w_ffn_out.astype(BF16)
    g_attn3 = g_attn.reshape(depth, 1, d)
    g_ffn3 = g_ffn.reshape(depth, 1, d)
    g_final2 = g_final.reshape(1, d)
    g2 = jnp.concatenate([subln_g, subln_g], axis=-1).reshape(depth, 1, LANES)
    pools = [_feature_major(c) for c in (cache_a_k, cache_a_v, cache_b_k, cache_b_v, cache_c_k, cache_c_v)]
    pool_kidx = _feature_major(cache_a_kidx)

    mod = _modulation(jnp.concatenate([c_prompt, c_sample], axis=0), w_mod, b_mod)
    mod_p = mod[:, :, :b].reshape(depth, 6, b, 1, d)
    mod_s = jnp.repeat(mod[:, :, b:], ts, axis=2).reshape(depth, 6, 1, nb * ts, d)

    tiles = _bias_tiles(rel_bias)

    def stacked(heads, rows):
        return jnp.concatenate([tiles[h][:, :rows] for h in heads], axis=1)

    pairs_a = [(2 * p, 2 * p + 1) for p in range(NPA)]
    quads_b = [(H_A + 2 * u,) * 2 + (H_A + 2 * u + 1,) * 2 for u in range(NPB)]
    ta_p = jnp.stack([stacked(hs, BLK) for hs in pairs_a])
    tb_p = jnp.stack([stacked(hs, BLK) for hs in quads_b])
    ta_s = jnp.stack([stacked(hs, SQ) for hs in pairs_a])
    tb_s = jnp.stack([stacked(hs, SQ) for hs in quads_b])

    xp = x_prompt.reshape(b * t, d)
    xs = x_sample.reshape(nb * ts, d)
    tm_p = _row_tile(t)
    tm_s = nb * ts
    new_p = {k: [] for k in _KV}
    new_s = {k: [] for k in _KV}

    def pad_q(a):
        return jnp.pad(a.reshape(nb, ts, -1), ((0, 0), (0, SQ - ts), (0, 0)))

    def new_keys(a, w=None):
        a = a[0] if w is None else a[0, :w]
        a = jnp.transpose(a.reshape(a.shape[0], nb, ts), (1, 0, 2))
        return jnp.pad(a, ((0, 0), (0, 0), (0, BLK - ts))).astype(BF16)

    for l in range(depth):
        lam_init = 0.8 - 0.6 * math.exp(-0.3 * l)
        final = l == depth - 1
        pr = _in_proj(xp, g_attn3, mod_p, wq, wk, l, b, tm_p, t // tm_p, True)
        mix = _attn_prompt(pr, ta_p, tb_p, diff_lambda, g2, l, b, t, topk_p, lam_init)
        xp = _out_proj(mix.reshape(b * t, d), xp, mod_p, w_out_b, l, tm_p, t // tm_p)
        xp = _ffn(xp, g_ffn3, mod_p, w_ffn_in_b, w_ffn_out_b, g_final2, l, tm_p, t // tm_p, final)
        sr = _in_proj(xs, g_attn3, mod_s, wq, wk, l, 1, tm_s, None, False)
        qi_stack = pad_q(sr["qi"]).reshape(nb, SQ, N_IDX_HEADS, IDX_DIM).transpose(0, 2, 1, 3)
        qi_stack = qi_stack.reshape(nb, N_IDX_HEADS * SQ, IDX_DIM)
        addmask = _select_sample(page_table, qi_stack, pad_q(sr["wi"]), new_keys(sr["ki"]),
                                 pool_kidx, l, g_sel, topk_s)
        mix_s = _attn_sample(page_table,
                             [pad_q(sr["qa"]), pad_q(sr["qd"]), pad_q(sr["qc"])],
                             [new_keys(sr[n]) for n in ("ka", "va", "kd", "vd", "kc", "vc")],
                             addmask, ta_s, tb_s, diff_lambda, g2, pools, l, g_att, lam_init)
        mix_s = mix_s[:, :ts].reshape(nb * ts, d)
        xs = _out_proj(mix_s, xs, mod_s, w_out_b, l, tm_s, None)
        xs = _ffn(xs, g_ffn3, mod_s, w_ffn_in_b, w_ffn_out_b, g_final2, l, tm_s, None, final)
        for k in _KV:
            new_p[k].append(pr[k])
            new_s[k].append(sr[k])

    tails = {"ka": (H_A, HEAD_DIM), "va": (H_A, HEAD_DIM), "ki": (IDX_DIM,), "kd": (H_B, 2, HD_B),
             "vd": (H_B, 2 * HD_B), "kc": (H_C, HEAD_DIM), "vc": (H_C, HEAD_DIM)}

    def rows_out(arrs, lead, tail):
        a = jnp.stack(arrs)
        nbt, tt = a.shape[1], a.shape[3]
        a = a.reshape((depth, nbt) + tail + (tt,))
        nd = a.ndim
        a = jnp.transpose(a, (0, 1, nd - 1) + tuple(range(2, nd - 1)))
        return a.reshape((depth,) + lead + tail)

    outs_p = [rows_out(new_p[k], (b, t), tails[k]) for k in _KV]
    outs_s = [rows_out(new_s[k], (nb, ts), tails[k]) for k in _KV]
    return (xp.reshape(b, t, d), xs.reshape(nb, ts, d), *outs_p, *outs_s)
```
